```python
import math
import jax, jax.numpy as jnp
from jax import lax
import numpy as np

D_MODEL = 4096
BATCH = 1
SEQ = 16384
DEPTH = 4

D_MIX = D_MODEL
D_HGRN = D_MIX // 2
D_DIFF = D_MIX - D_HGRN
HG_HEAD_DIM = 128
HG_HEADS = D_HGRN // HG_HEAD_DIM
HG_CHUNK = 64
DA_HEAD_DIM = 128
DA_V_DIM = 2 * DA_HEAD_DIM
DA_HEADS = D_DIFF // DA_V_DIM
Q_BLOCK = 128
IN_COLS = 5 * D_HGRN + 3 * D_DIFF
D_FF = 2 * D_MODEL
CONV_WIDTH = 3
N_MOD = 6
EPS = 1e-6

kernel_name = "hymba_hgrn2_diffattn_convglu_adaln_encoder"


def rmsnorm(x, g):
    x32 = x.astype(jnp.float32)
    y = x32 * lax.rsqrt(jnp.mean(x32 * x32, axis=-1, keepdims=True) + EPS)
    return (y * g.astype(jnp.float32)).astype(x.dtype)


def modulate(h, shift, scale):
    return h * (1.0 + scale) + shift


def alibi_slopes(n_heads):
    return jnp.exp2(-8.0 * jnp.arange(1, n_heads + 1, dtype=jnp.float32) / n_heads)


def forget_log(z, lb):
    B, T, _ = z.shape
    z = z.reshape(B, T, HG_HEADS, HG_HEAD_DIM).astype(jnp.float32)
    lbh = lb.reshape(HG_HEADS, HG_HEAD_DIM)
    return jnp.log(lbh + (1.0 - lbh) * jax.nn.sigmoid(z))


def hgrn2_scan(q, v, logf):
    B, T, H, K = q.shape
    V = v.shape[-1]
    N = T // HG_CHUNK

    def to_chunks(a):
        return a.reshape(B, N, HG_CHUNK, H, a.shape[-1]).transpose(1, 0, 3, 2, 4)

    mask = jnp.tril(jnp.ones((HG_CHUNK, HG_CHUNK), dtype=bool))

    def step(S, inp):
        qc, vc, gc = inp
        kc = -jnp.expm1(gc)
        b = jnp.cumsum(gc, axis=2)
        bl = b[:, :, -1:, :]
        o_inter = jnp.einsum('bhtk,bhkv->bhtv', qc * jnp.exp(b), S)
        rel = jnp.where(mask[:, :, None], b[:, :, :, None, :] - b[:, :, None, :, :], -jnp.inf)
        a = jnp.einsum('bhtk,bhtsk,bhsk->bhts', qc, jnp.exp(rel), kc)
        o = o_inter + jnp.einsum('bhts,bhsv->bhtv', a, vc)
        S = jnp.exp(bl[:, :, 0, :])[..., None] * S + jnp.einsum('bhsk,bhsv->bhkv', kc * jnp.exp(bl - b), vc)
        return S, o

    S0 = jnp.zeros((B, H, K, V), jnp.float32)
    _, o = lax.scan(step, S0, (to_chunks(q), to_chunks(v), to_chunks(logf)))
    return o.transpose(1, 0, 3, 2, 4).reshape(B, T, H, V)


def hgrn2_group(hq, hf_f, hf_b, hi, hg, lb_f, lb_b, norm_g):
    B, T, _ = hq.shape
    shp = (B, T, HG_HEADS, HG_HEAD_DIM)
    q = hq.reshape(shp).astype(jnp.float32) * (HG_HEAD_DIM ** -0.5)
    v = hi.reshape(shp).astype(jnp.float32)
    o_fwd = hgrn2_scan(q, v, forget_log(hf_f, lb_f))
    o_bwd = hgrn2_scan(q[:, ::-1], v[:, ::-1], forget_log(hf_b, lb_b)[:, ::-1])[:, ::-1]
    o = rmsnorm(o_fwd + o_bwd, norm_g) * jax.nn.silu(hg.reshape(shp).astype(jnp.float32))
    return o.reshape(B, T, D_HGRN).astype(hq.dtype)


def diff_attention_group(dq, dk, dv, lam_params, norm_g, layer_idx):
    B, T, _ = dq.shape
    q = dq.reshape(B, T, DA_HEADS, 2, DA_HEAD_DIM).astype(jnp.float32) * (DA_HEAD_DIM ** -0.5)
    k = dk.reshape(B, T, DA_HEADS, 2, DA_HEAD_DIM).astype(jnp.float32)
    v = dv.reshape(B, T, DA_HEADS, DA_V_DIM).astype(jnp.float32)
    lam_init = 0.8 - 0.6 * math.exp(-0.3 * layer_idx)
    lp = lam_params.astype(jnp.float32)
    lam = jnp.exp(jnp.sum(lp[0] * lp[1])) - jnp.exp(jnp.sum(lp[2] * lp[3])) + lam_init
    slopes = alibi_slopes(DA_HEADS)
    nb = T // Q_BLOCK
    pos = jnp.arange(T, dtype=jnp.int32)
    qb = q.reshape(B, nb, Q_BLOCK, DA_HEADS, 2, DA_HEAD_DIM).transpose(1, 0, 3, 4, 2, 5)
    kt = k.transpose(0, 2, 3, 1, 4)
    vt = v.transpose(0, 2, 1, 3)
    qpos = pos.reshape(nb, Q_BLOCK)

    def block(args):
        qblk, tq = args
        s = jnp.einsum('bhjqd,bhjkd->bhjqk', qblk, kt)
        dist = jnp.abs(tq[:, None] - pos[None, :]).astype(jnp.float32)
        s = s - slopes[None, :, None, None, None] * dist
        p = jax.nn.softmax(s, axis=-1)
        a = p[:, :, 0] - lam * p[:, :, 1]
        return jnp.einsum('bhqk,bhkv->bhqv', a, vt)

    o = lax.map(block, (qb, qpos))
    o = o.transpose(1, 0, 3, 2, 4).reshape(B, T, DA_HEADS, DA_V_DIM)
    o = rmsnorm(o, norm_g) * (1.0 - lam_init)
    return o.reshape(B, T, D_DIFF).astype(dq.dtype)


def token_mixer(h, w_in_l, lb_f, lb_b, hg_norm_g_l, lam_l, da_norm_g_l, w_out_l, layer_idx):
    proj = h @ w_in_l
    splits = [D_HGRN, 2 * D_HGRN, 3 * D_HGRN, 4 * D_HGRN, 5 * D_HGRN,
              5 * D_HGRN + D_DIFF, 5 * D_HGRN + 2 * D_DIFF]
    hq, hf_f, hf_b, hi, hg, dq, dk, dv = jnp.split(proj, splits, axis=-1)
    o_hgrn = hgrn2_group(hq, hf_f, hf_b, hi, hg, lb_f, lb_b, hg_norm_g_l)
    o_diff = diff_attention_group(dq, dk, dv, lam_l, da_norm_g_l, layer_idx)
    return jnp.concatenate([o_hgrn, o_diff], axis=-1) @ w_out_l


def conv_glu(h, w_up_l, cw, cb, w_down_l):
    u = h @ w_up_l
    up = jnp.pad(u, ((0, 0), (1, 1), (0, 0)))
    u = up[:, :-2] * cw[0] + up[:, 1:-1] * cw[1] + up[:, 2:] * cw[2] + cb
    gate, val = jnp.split(u, 2, axis=-1)
    return (jax.nn.silu(gate) * val) @ w_down_l


def setup_inputs(seed: int = 0) -> dict:
    key = jax.random.key(seed)
    ks = jax.random.split(key, 18)

    def nrm(k, shape, scale):
        return jax.random.normal(k, shape, jnp.float32) * scale

    return {
        "x": nrm(ks[0], (BATCH, SEQ, D_MODEL), 1.0),
        "c": nrm(ks[1], (BATCH, D_MODEL), 1.0),
        "w_ada": nrm(ks[2], (D_MODEL, N_MOD * D_MODEL), 0.5 * D_MODEL ** -0.5),
        "b_ada": nrm(ks[3], (N_MOD * D_MODEL,), 0.01),
        "ada_table": nrm(ks[4], (DEPTH, N_MOD, D_MODEL), 0.1),
        "norm1_g": 1.0 + nrm(ks[5], (DEPTH, D_MODEL), 0.02),
        "w_in": nrm(ks[6], (DEPTH, D_MODEL, IN_COLS), D_MODEL ** -0.5),
        "hg_lb_logits": nrm(ks[7], (2, DEPTH, D_HGRN), 0.5),
        "hg_norm_g": 1.0 + nrm(ks[8], (DEPTH, HG_HEAD_DIM), 0.02),
        "da_lambda": nrm(ks[9], (DEPTH, 4, DA_HEAD_DIM), 0.1),
        "da_norm_g": 1.0 + nrm(ks[10], (DEPTH, DA_V_DIM), 0.02),
        "w_out": nrm(ks[11], (DEPTH, D_MIX, D_MODEL), D_MIX ** -0.5),
        "norm2_g": 1.0 + nrm(ks[12], (DEPTH, D_MODEL), 0.02),
        "w_up": nrm(ks[13], (DEPTH, D_MODEL, 2 * D_FF), D_MODEL ** -0.5),
        "conv_w": nrm(ks[14], (DEPTH, CONV_WIDTH, 2 * D_FF), CONV_WIDTH ** -0.5),
        "conv_b": nrm(ks[15], (DEPTH, 2 * D_FF), 0.02),
        "w_down": nrm(ks[16], (DEPTH, D_FF, D_MODEL), D_FF ** -0.5),
        "final_g": 1.0 + nrm(ks[17], (D_MODEL,), 0.02),
    }


def reference(x, c, w_ada, b_ada, ada_table, norm1_g, w_in, hg_lb_logits, hg_norm_g, da_lambda,
              da_norm_g, w_out, norm2_g, w_up, conv_w, conv_b, w_down, final_g):
    B, T, D = x.shape
    mod = (jax.nn.silu(c) @ w_ada + b_ada).reshape(B, N_MOD, D)
    p = jax.nn.softmax(hg_lb_logits.astype(jnp.float32), axis=1)
    lb = jnp.cumsum(p, axis=1)
    lb = lb - lb[:, :1]
    for l in range(DEPTH):
        m = mod + ada_table[l]
        shift1, scale1, gate1 = m[:, 0, None, :], m[:, 1, None, :], m[:, 2, None, :]
        shift2, scale2, gate2 = m[:, 3, None, :], m[:, 4, None, :], m[:, 5, None, :]
        h = modulate(rmsnorm(x, norm1_g[l]), shift1, scale1)
        x = x + gate1 * token_mixer(h, w_in[l], lb[0, l], lb[1, l], hg_norm_g[l], da_lambda[l],
                                    da_norm_g[l], w_out[l], l)
        h = modulate(rmsnorm(x, norm2_g[l]), shift2, scale2)
        x = x + gate2 * conv_glu(h, w_up[l], conv_w[l], conv_b[l], w_down[l])
    return rmsnorm(x, final_g)
```

```python
import functools
import math

import jax
import jax.numpy as jnp
import numpy as np
from jax import lax
from jax.experimental import pallas as pl
from jax.experimental.pallas import tpu as pltpu

F32 = jnp.float32
BF16 = jnp.bfloat16

HG_HEAD_DIM = 128
DA_HEAD_DIM = 128
DA_V_DIM = 2 * DA_HEAD_DIM
N_MOD = 6
CONV_WIDTH = 3
EPS = 1e-6
HG_LEAF = 16
HG_LEAF_CLAMP = 80.0
VMEM_LIMIT_BYTES = 56 * 1024 * 1024


def _params(*semantics):
    return pltpu.CompilerParams(dimension_semantics=semantics, vmem_limit_bytes=VMEM_LIMIT_BYTES)


def _dot(a, b):
    return lax.dot_general(a, b, (((1,), (0,)), ((), ())), preferred_element_type=F32)


def _dot_nt(a, b):
    return lax.dot_general(a, b, (((1,), (1,)), ((), ())), preferred_element_type=F32)


def _dot_tn(a, b):
    return lax.dot_general(a, b, (((0,), (0,)), ((), ())), preferred_element_type=F32)


def _sigmoid(x):
    return 1.0 / (1.0 + jnp.exp(-x))


def _pick(n, *cands):
    for c in cands:
        if n % c == 0:
            return c
    return n


def _mod_kernel(c_ref, w_ref, b_ref, t_ref, o_ref):
    c = c_ref[...]
    s = c * _sigmoid(c)
    y = jnp.dot(s, w_ref[...], preferred_element_type=F32, precision=lax.Precision.HIGHEST)
    o_ref[...] = y[0:1, :] + b_ref[...] + t_ref[...]


def _ada_mod(c, w_ada, b_ada, ada_table):
    _, d = c.shape
    depth = ada_table.shape[0]
    n = w_ada.shape[1]
    tn = _pick(n, 512, 256, 128)
    c8 = jnp.broadcast_to(c, (8, d))
    return pl.pallas_call(
        _mod_kernel,
        grid=(n // tn,),
        in_specs=[pl.BlockSpec((8, d), lambda j: (0, 0)),
                  pl.BlockSpec((d, tn), lambda j: (0, j)),
                  pl.BlockSpec((1, tn), lambda j: (0, j)),
                  pl.BlockSpec((depth, tn), lambda j: (0, j))],
        out_specs=pl.BlockSpec((depth, tn), lambda j: (0, j)),
        out_shape=jax.ShapeDtypeStruct((depth, n), F32),
        compiler_params=_params("parallel"),
    )(c8, w_ada, b_ada.reshape(1, n), ada_table.reshape(depth, n))


def _norm_kernel(x_ref, g_ref, *rest, modulated):
    if modulated:
        shift_ref, scale_ref, o_ref = rest
    else:
        (o_ref,) = rest
    x = x_ref[...]
    y = x * lax.rsqrt(jnp.mean(x * x, axis=-1, keepdims=True) + EPS) * g_ref[...]
    if modulated:
        y = y * (1.0 + scale_ref[...]) + shift_ref[...]
    o_ref[...] = y.astype(o_ref.dtype)


def _rmsnorm(x, g, shift=None, scale=None, out_dtype=BF16):
    t, d = x.shape
    tr = _pick(t, 256, 128, 64, 32, 16, 8)
    modulated = shift is not None
    row = pl.BlockSpec((1, d), lambda i: (0, 0))
    args = [x, g.reshape(1, d)] + ([shift.reshape(1, d), scale.reshape(1, d)] if modulated else [])
    return pl.pallas_call(
        functools.partial(_norm_kernel, modulated=modulated),
        grid=(t // tr,),
        in_specs=[pl.BlockSpec((tr, d), lambda i: (i, 0))] + [row] * (len(args) - 1),
        out_specs=pl.BlockSpec((tr, d), lambda i: (i, 0)),
        out_shape=jax.ShapeDtypeStruct((t, d), out_dtype),
        compiler_params=_params("parallel"),
    )(*args)


def _matmul_kernel(*refs, n_a, steps_per_a, residual):
    a_refs = refs[:n_a]
    w_ref = refs[n_a]
    pos = n_a + 1
    if residual:
        x_ref, gate_ref = refs[pos], refs[pos + 1]
        pos += 2
    o_ref = refs[pos]
    acc_ref = refs[pos + 1] if len(refs) > pos + 1 else None
    nk = n_a * steps_per_a
    k = pl.program_id(2)

    def finish(acc):
        if residual:
            o_ref[...] = x_ref[...] + gate_ref[...] * acc
        else:
            o_ref[...] = acc.astype(o_ref.dtype)

    if nk == 1:
        finish(_dot(a_refs[0][...], w_ref[...]))
        return

    for p in range(n_a):
        @pl.when(k // steps_per_a == p)
        def _(p=p):
            part = _dot(a_refs[p][...], w_ref[...])

            @pl.when(k == 0)
            def _():
                acc_ref[...] = part

            @pl.when(k > 0)
            def _():
                acc_ref[...] += part

    @pl.when(k == nk - 1)
    def _():
        finish(acc_ref[...])


def _matmul(a_list, w, *, tm, tn, tk, out_dtype=BF16, x=None, gate=None):
    m, ka = a_list[0].shape
    n_a = len(a_list)
    kt, n = w.shape
    assert kt == n_a * ka and all(a.shape == (m, ka) for a in a_list)
    tm, tn, tk = min(tm, m), min(tn, n), min(tk, ka)
    assert m % tm == 0 and n % tn == 0 and ka % tk == 0
    spa = ka // tk
    nk = n_a * spa
    residual = x is not None

    def a_map(p):
        return lambda i, j, k: (i, jnp.clip(k - p * spa, 0, spa - 1))

    in_specs = [pl.BlockSpec((tm, tk), a_map(p)) for p in range(n_a)]
    in_specs.append(pl.BlockSpec((tk, tn), lambda i, j, k: (k, j)))
    args = list(a_list) + [w]
    if residual:
        in_specs += [pl.BlockSpec((tm, tn), lambda i, j, k: (i, j)),
                     pl.BlockSpec((1, tn), lambda i, j, k: (0, j))]
        args += [x, gate.reshape(1, n)]
        out_dtype = F32
    return pl.pallas_call(
        functools.partial(_matmul_kernel, n_a=n_a, steps_per_a=spa, residual=residual),
        grid=(m // tm, n // tn, nk),
        in_specs=in_specs,
        out_specs=pl.BlockSpec((tm, tn), lambda i, j, k: (i, j)),
        out_shape=jax.ShapeDtypeStruct((m, n), out_dtype),
        scratch_shapes=[pltpu.VMEM((tm, tn), F32)] if nk > 1 else [],
        compiler_params=_params("parallel", "parallel", "arbitrary"),
    )(*args)


def _hgrn_level_ids(r, reverse):
    t = np.arange(r)[:, None]
    s = np.arange(r)[None, :]
    ids = np.full((r, r), -1, np.int32)
    tri = (s >= t) if reverse else (s <= t)
    ids[(t // HG_LEAF == s // HG_LEAF) & tri] = 0
    lv, i = HG_LEAF, 1
    while 2 * lv <= r:
        same = (t // (2 * lv)) == (s // (2 * lv))
        t_hi, s_hi = (t & lv) != 0, (s & lv) != 0
        pair = (~t_hi & s_hi) if reverse else (t_hi & ~s_hi)
        ids[same & pair] = i
        lv, i = 2 * lv, i + 1
    return ids


def _hgrn_kernel(q_ref, z_ref, v_ref, lbl_ref, ids_ref, *rest, reverse, layer, heads, final):
    if final:
        ofwd_ref, gate_ref, ng_ref, o_ref, st_ref = rest
    else:
        o_ref, st_ref = rest
    r = q_ref.shape[0]
    hd = HG_HEAD_DIM

    @pl.when(pl.program_id(1) == 0)
    def _():
        st_ref[...] = jnp.zeros_like(st_ref)

    logits = lbl_ref[0]
    e = jnp.exp(logits - jnp.max(logits, axis=0, keepdims=True))
    lb_all = (jnp.sum(e[1:layer + 1], axis=0, keepdims=True) / jnp.sum(e, axis=0, keepdims=True)
              if layer > 0 else jnp.zeros_like(e[0:1]))

    ids = ids_ref[...]
    row = lax.broadcasted_iota(jnp.int32, (r, r), 0)
    col = lax.broadcasted_iota(jnp.int32, (r, r), 1)
    tri = jnp.where((col >= row) if reverse else (col <= row), 1.0, 0.0).astype(BF16)
    total_row = 0 if reverse else r - 1

    for h in range(heads):
        sl = slice(h * hd, (h + 1) * hd)
        lb = lb_all[:, sl]
        q = q_ref[:, sl].astype(F32) * (hd ** -0.5)
        z = z_ref[:, sl].astype(F32)
        v = v_ref[:, sl]
        sig = _sigmoid(z)
        g = jnp.log(lb + (1.0 - lb) * sig)
        kc = (1.0 - lb) * (1.0 - sig)

        g1 = g.astype(BF16)
        rem = g - g1.astype(F32)
        g2 = rem.astype(BF16)
        g3 = (rem - g2.astype(F32)).astype(BF16)
        b = _dot(tri, g1) + _dot(tri, g2) + _dot(tri, g3)

        a = jnp.zeros((r, r), F32)
        lv, level = HG_LEAF // 2, 0
        while 2 * lv <= r:
            blk = 2 * lv
            nb = r // blk
            ref_row = lv if reverse else lv - 1
            bb = b.reshape(nb, blk, hd)
            d = bb - bb[:, ref_row:ref_row + 1, :]
            cap = HG_LEAF_CLAMP if level == 0 else 0.0
            qe = (q.reshape(nb, blk, hd) * jnp.exp(jnp.minimum(d, cap))).reshape(r, hd).astype(BF16)
            ke = (kc.reshape(nb, blk, hd) * jnp.exp(jnp.minimum(-d, cap))).reshape(r, hd).astype(BF16)
            a = jnp.where(ids == level, _dot_nt(qe, ke), a)
            lv, level = 2 * lv, level + 1

        st = st_ref[h]
        o = _dot(a.astype(BF16), v) + _dot_nt((q * jnp.exp(b)).astype(BF16), st.astype(BF16))
        bl = b[total_row:total_row + 1, :]
        kd = (kc * jnp.exp(bl - b)).astype(BF16)
        st_ref[h] = jnp.exp(bl) * st + _dot_tn(v, kd)

        if final:
            o = o + ofwd_ref[:, sl]
            y = o * lax.rsqrt(jnp.mean(o * o, axis=-1, keepdims=True) + EPS) * ng_ref[...]
            gt = gate_ref[:, sl].astype(F32)
            o_ref[:, sl] = (y * (gt * _sigmoid(gt))).astype(o_ref.dtype)
        else:
            o_ref[:, sl] = o


def _hgrn_pass(proj, lb_logits, layer, d_hgrn, *, reverse, o_fwd=None, norm_g=None, chunk=256):
    t = proj.shape[0]
    heads_total = d_hgrn // HG_HEAD_DIM
    hb = _pick(heads_total, 4, 2, 1)
    w = hb * HG_HEAD_DIM
    r = min(chunk, t)
    assert t % r == 0 and r % (2 * HG_LEAF) == 0
    nchunks = t // r
    cpb = d_hgrn // w
    final = o_fwd is not None
    depth = lb_logits.shape[1]

    def rows(n):
        return (nchunks - 1 - n) if reverse else n

    def col_spec(group):
        return pl.BlockSpec((r, w), lambda h, n: (rows(n), group * cpb + h))

    in_specs = [col_spec(0), col_spec(2 if reverse else 1), col_spec(3),
                pl.BlockSpec((1, depth, w), lambda h, n: (1 if reverse else 0, 0, h)),
                pl.BlockSpec((r, r), lambda h, n: (0, 0))]
    args = [proj, proj, proj, lb_logits, jnp.asarray(_hgrn_level_ids(r, reverse))]
    if final:
        in_specs += [pl.BlockSpec((r, w), lambda h, n: (rows(n), h)), col_spec(4),
                     pl.BlockSpec((1, HG_HEAD_DIM), lambda h, n: (0, 0))]
        args += [o_fwd, proj, norm_g.reshape(1, HG_HEAD_DIM)]
    return pl.pallas_call(
        functools.partial(_hgrn_kernel, reverse=reverse, layer=layer, heads=hb, final=final),
        grid=(heads_total // hb, nchunks),
        in_specs=in_specs,
        out_specs=pl.BlockSpec((r, w), lambda h, n: (rows(n), h)),
        out_shape=jax.ShapeDtypeStruct((t, d_hgrn), BF16 if final else F32),
        scratch_shapes=[pltpu.VMEM((hb, HG_HEAD_DIM, HG_HEAD_DIM), F32)],
        compiler_params=_params("parallel", "arbitrary"),
    )(*args)


def _diff_attn_kernel(q_ref, k_ref, v_ref, slope_ref, lam_ref, ng_ref, o_ref,
                      qs_ref, m_ref, l_ref, acc_ref, *, lam_init):
    i, j = pl.program_id(1), pl.program_id(2)
    tq, tk = q_ref.shape[0], k_ref.shape[0]
    hd = DA_HEAD_DIM

    @pl.when(j == 0)
    def _():
        qs_ref[...] = (q_ref[...].astype(F32) * (hd ** -0.5)).astype(BF16)
        m_ref[...] = jnp.full_like(m_ref, -jnp.inf)
        l_ref[...] = jnp.zeros_like(l_ref)
        acc_ref[...] = jnp.zeros_like(acc_ref)

    slope = slope_ref[0][0:1, 0:1]
    tpos = i * tq + lax.broadcasted_iota(jnp.int32, (tq, tk), 0)
    spos = j * tk + lax.broadcasted_iota(jnp.int32, (tq, tk), 1)
    bias = slope * jnp.abs(tpos - spos).astype(F32)
    v = v_ref[...]
    for mp in range(2):
        sl = slice(mp * hd, (mp + 1) * hd)
        s = _dot_nt(qs_ref[:, sl], k_ref[:, sl]) - bias
        m_old = m_ref[mp]
        m_new = jnp.maximum(m_old, jnp.max(s, axis=-1, keepdims=True))
        alpha = jnp.exp(m_old - m_new)
        p = jnp.exp(s - m_new)
        l_ref[mp] = alpha * l_ref[mp] + jnp.sum(p, axis=-1, keepdims=True)
        acc_ref[mp] = alpha * acc_ref[mp] + _dot(p.astype(BF16), v)
        m_ref[mp] = m_new

    @pl.when(j == pl.num_programs(2) - 1)
    def _():
        lp = lam_ref[...]
        lam = (jnp.exp(jnp.sum(lp[0:1] * lp[1:2], axis=-1, keepdims=True))
               - jnp.exp(jnp.sum(lp[2:3] * lp[3:4], axis=-1, keepdims=True)) + lam_init)
        o = acc_ref[0] / l_ref[0] - lam * (acc_ref[1] / l_ref[1])
        y = o * lax.rsqrt(jnp.mean(o * o, axis=-1, keepdims=True) + EPS) * ng_ref[...]
        o_ref[...] = (y * (1.0 - lam_init)).astype(o_ref.dtype)


def _diff_attention(proj, lam_params, norm_g, layer, d_hgrn, d_diff, *, tq=512, tk=512):
    t = proj.shape[0]
    heads = d_diff // DA_V_DIM
    tq, tk = min(tq, t), min(tk, t)
    assert t % tq == 0 and t % tk == 0
    lam_init = 0.8 - 0.6 * math.exp(-0.3 * layer)
    base = 5 * d_hgrn // DA_V_DIM
    slopes = np.exp2(-8.0 * np.arange(1, heads + 1, dtype=np.float32) / heads).astype(np.float32)
    slopes = jnp.asarray(np.broadcast_to(slopes[:, None, None], (heads, 1, 128)))
    return pl.pallas_call(
        functools.partial(_diff_attn_kernel, lam_init=lam_init),
        grid=(heads, t // tq, t // tk),
        in_specs=[pl.BlockSpec((tq, DA_V_DIM), lambda h, i, j: (i, base + h)),
                  pl.BlockSpec((tk, DA_V_DIM), lambda h, i, j: (j, base + heads + h)),
                  pl.BlockSpec((tk, DA_V_DIM), lambda h, i, j: (j, base + 2 * heads + h)),
                  pl.BlockSpec((1, 1, 128), lambda h, i, j: (h, 0, 0)),
                  pl.BlockSpec((4, DA_HEAD_DIM), lambda h, i, j: (0, 0)),
                  pl.BlockSpec((1, DA_V_DIM), lambda h, i, j: (0, 0))],
        out_specs=pl.BlockSpec((tq, DA_V_DIM), lambda h, i, j: (i, h)),
        out_shape=jax.ShapeDtypeStruct((t, d_diff), BF16),
        scratch_shapes=[pltpu.VMEM((tq, DA_V_DIM), BF16),
                        pltpu.VMEM((2, tq, 1), F32),
                        pltpu.VMEM((2, tq, 1), F32),
                        pltpu.VMEM((2, tq, DA_V_DIM), F32)],
        compiler_params=_params("parallel", "parallel", "arbitrary"),
    )(proj, proj, proj, slopes, lam_params, norm_g.reshape(1, DA_V_DIM))


def _conv_glu_kernel(ug_ref, uv_ref, pg_ref, pv_ref, ng_ref, nv_ref, cwg_ref, cwv_ref, cbg_ref, cbv_ref, o_ref):
    i = pl.program_id(0)
    tr = ug_ref.shape[0]
    has_prev = jnp.where(i > 0, 1.0, 0.0)
    has_next = jnp.where(i < pl.num_programs(0) - 1, 1.0, 0.0)
    row = lax.broadcasted_iota(jnp.int32, ug_ref.shape, 0)

    def conv(u_ref, p_ref, n_ref, cw_ref, cb_ref):
        u = u_ref[...].astype(F32)
        prev = p_ref[7:8, :].astype(F32) * has_prev
        nxt = n_ref[0:1, :].astype(F32) * has_next
        up = jnp.where(row == 0, prev, pltpu.roll(u, 1, axis=0))
        un = jnp.where(row == tr - 1, nxt, pltpu.roll(u, tr - 1, axis=0))
        cw = cw_ref[...]
        return up * cw[0:1] + u * cw[1:2] + un * cw[2:3] + cb_ref[...]

    gate = conv(ug_ref, pg_ref, ng_ref, cwg_ref, cbg_ref)
    val = conv(uv_ref, pv_ref, nv_ref, cwv_ref, cbv_ref)
    o_ref[...] = (gate * _sigmoid(gate) * val).astype(o_ref.dtype)


def _conv_glu(u, conv_w, conv_b, *, tr=256, tc=1024):
    t, n2 = u.shape
    dff = n2 // 2
    tr, tc = min(tr, t), min(tc, dff)
    assert t % tr == 0 and dff % tc == 0 and tr % 8 == 0
    ncb = dff // tc
    r8 = tr // 8
    last8 = t // 8 - 1

    def main(off):
        return pl.BlockSpec((tr, tc), lambda i, j: (i, off + j))

    def prev(off):
        return pl.BlockSpec((8, tc), lambda i, j: (jnp.maximum(i * r8 - 1, 0), off + j))

    def nxt(off):
        return pl.BlockSpec((8, tc), lambda i, j: (jnp.minimum((i + 1) * r8, last8), off + j))

    def prm(rows, off):
        return pl.BlockSpec((rows, tc), lambda i, j: (0, off + j))

    cb = conv_b.reshape(1, n2)
    return pl.pallas_call(
        _conv_glu_kernel,
        grid=(t // tr, ncb),
        in_specs=[main(0), main(ncb), prev(0), prev(ncb), nxt(0), nxt(ncb),
                  prm(CONV_WIDTH, 0), prm(CONV_WIDTH, ncb), prm(1, 0), prm(1, ncb)],
        out_specs=pl.BlockSpec((tr, tc), lambda i, j: (i, j)),
        out_shape=jax.ShapeDtypeStruct((t, dff), BF16),
        compiler_params=_params("parallel", "parallel"),
    )(u, u, u, u, u, u, conv_w, conv_w, cb, cb)


def kernel(x, c, w_ada, b_ada, ada_table, norm1_g, w_in, hg_lb_logits, hg_norm_g, da_lambda, da_norm_g,
           w_out, norm2_g, w_up, conv_w, conv_b, w_down, final_g):
    bsz, t, d = x.shape
    assert bsz == 1
    depth = w_in.shape[0]
    d_hgrn = hg_lb_logits.shape[-1]
    d_diff = w_out.shape[1] - d_hgrn
    assert w_in.shape[2] == 5 * d_hgrn + 3 * d_diff

    mod = _ada_mod(c, w_ada, b_ada, ada_table)
    xs = x.reshape(t, d)
    for l in range(depth):
        m = [mod[l, i * d:(i + 1) * d] for i in range(N_MOD)]
        h = _rmsnorm(xs, norm1_g[l], m[0], m[1])
        proj = _matmul([h], w_in[l].astype(BF16), tm=1024, tn=1024, tk=d)
        o_fwd = _hgrn_pass(proj, hg_lb_logits, l, d_hgrn, reverse=False)
        o_hg = _hgrn_pass(proj, hg_lb_logits, l, d_hgrn, reverse=True, o_fwd=o_fwd, norm_g=hg_norm_g[l])
        o_da = _diff_attention(proj, da_lambda[l], da_norm_g[l], l, d_hgrn, d_diff)
        if d_hgrn == d_diff:
            mix = [o_hg, o_da]
        else:
            mix = [jnp.concatenate([o_hg, o_da], axis=1)]
        xs = _matmul(mix, w_out[l].astype(BF16), tm=1024, tn=512, tk=2048, x=xs, gate=m[2])
        h = _rmsnorm(xs, norm2_g[l], m[3], m[4])
        u = _matmul([h], w_up[l].astype(BF16), tm=1024, tn=1024, tk=d)
        act = _conv_glu(u, conv_w[l], conv_b[l])
        xs = _matmul([act], w_down[l].astype(BF16), tm=1024, tn=1024, tk=2048, x=xs, gate=m[5])
    out = _rmsnorm(xs, final_g, out_dtype=F32)
    return out.reshape(bsz, t, d)
```

```python
import functools
import math

import jax
import jax.numpy as jnp
import numpy as np
from jax import lax
from jax.experimental import pallas as pl
from jax.experimental.pallas import tpu as pltpu

F32 = jnp.float32
BF16 = jnp.bfloat16

HG_HEAD_DIM = 128
DA_HEAD_DIM = 128
DA_V_DIM = 2 * DA_HEAD_DIM
N_MOD = 6
CONV_WIDTH = 3
EPS = 1e-6
HG_LEAF = 16
HG_LEAF_CLAMP = 80.0
LOG2E = 1.4426950408889634
DA_SKIP_MARGIN = 160.0
VMEM_LIMIT_BYTES = 56 * 1024 * 1024


def _params(*semantics):
    return pltpu.CompilerParams(dimension_semantics=semantics, vmem_limit_bytes=VMEM_LIMIT_BYTES)


def _dot(a, b):
    return lax.dot_general(a, b, (((1,), (0,)), ((), ())), preferred_element_type=F32)


def _dot_nt(a, b):
    return lax.dot_general(a, b, (((1,), (1,)), ((), ())), preferred_element_type=F32)


def _dot_tn(a, b):
    return lax.dot_general(a, b, (((0,), (0,)), ((), ())), preferred_element_type=F32)


def _sigmoid(x):
    return 1.0 / (1.0 + jnp.exp(-x))


def _pick(n, *cands):
    for c in cands:
        if n % c == 0:
            return c
    return n


def _mod_kernel(c_ref, w_ref, b_ref, t_ref, o_ref):
    c = c_ref[...]
    s = c * _sigmoid(c)
    y = jnp.dot(s, w_ref[...], preferred_element_type=F32, precision=lax.Precision.HIGHEST)
    o_ref[...] = y[0:1, :] + b_ref[...] + t_ref[...]


def _ada_mod(c, w_ada, b_ada, ada_table):
    _, d = c.shape
    depth = ada_table.shape[0]
    n = w_ada.shape[1]
    tn = _pick(n, 512, 256, 128)
    c8 = jnp.broadcast_to(c, (8, d))
    return pl.pallas_call(
        _mod_kernel,
        grid=(n // tn,),
        in_specs=[pl.BlockSpec((8, d), lambda j: (0, 0)),
                  pl.BlockSpec((d, tn), lambda j: (0, j)),
                  pl.BlockSpec((1, tn), lambda j: (0, j)),
                  pl.BlockSpec((depth, tn), lambda j: (0, j))],
        out_specs=pl.BlockSpec((depth, tn), lambda j: (0, j)),
        out_shape=jax.ShapeDtypeStruct((depth, n), F32),
        compiler_params=_params("parallel"),
    )(c8, w_ada, b_ada.reshape(1, n), ada_table.reshape(depth, n))


def _norm_kernel(x_ref, g_ref, *rest, modulated):
    if modulated:
        shift_ref, scale_ref, o_ref = rest
    else:
        (o_ref,) = rest
    x = x_ref[...]
    y = x * lax.rsqrt(jnp.mean(x * x, axis=-1, keepdims=True) + EPS) * g_ref[...]
    if modulated:
        y = y * (1.0 + scale_ref[...]) + shift_ref[...]
    o_ref[...] = y.astype(o_ref.dtype)


def _rmsnorm(x, g, shift=None, scale=None, out_dtype=BF16):
    t, d = x.shape
    tr = _pick(t, 256, 128, 64, 32, 16, 8)
    modulated = shift is not None
    row = pl.BlockSpec((1, d), lambda i: (0, 0))
    args = [x, g.reshape(1, d)] + ([shift.reshape(1, d), scale.reshape(1, d)] if modulated else [])
    return pl.pallas_call(
        functools.partial(_norm_kernel, modulated=modulated),
        grid=(t // tr,),
        in_specs=[pl.BlockSpec((tr, d), lambda i: (i, 0))] + [row] * (len(args) - 1),
        out_specs=pl.BlockSpec((tr, d), lambda i: (i, 0)),
        out_shape=jax.ShapeDtypeStruct((t, d), out_dtype),
        compiler_params=_params("parallel"),
    )(*args)


def _matmul_kernel(*refs, n_a, steps_per_a, residual):
    a_refs = refs[:n_a]
    w_ref = refs[n_a]
    pos = n_a + 1
    if residual:
        x_ref, gate_ref = refs[pos], refs[pos + 1]
        pos += 2
    o_ref = refs[pos]
    acc_ref = refs[pos + 1] if len(refs) > pos + 1 else None
    nk = n_a * steps_per_a
    k = pl.program_id(2)

    def finish(acc):
        if residual:
            o_ref[...] = x_ref[...] + gate_ref[...] * acc
        else:
            o_ref[...] = acc.astype(o_ref.dtype)

    if nk == 1:
        finish(_dot(a_refs[0][...], w_ref[...]))
        return

    for p in range(n_a):
        @pl.when(k // steps_per_a == p)
        def _(p=p):
            part = _dot(a_refs[p][...], w_ref[...])

            @pl.when(k == 0)
            def _():
                acc_ref[...] = part

            @pl.when(k > 0)
            def _():
                acc_ref[...] += part

    @pl.when(k == nk - 1)
    def _():
        finish(acc_ref[...])


def _matmul(a_list, w, *, tm, tn, tk, out_dtype=BF16, x=None, gate=None):
    m, ka = a_list[0].shape
    n_a = len(a_list)
    kt, n = w.shape
    assert kt == n_a * ka and all(a.shape == (m, ka) for a in a_list)
    tm, tn, tk = min(tm, m), min(tn, n), min(tk, ka)
    assert m % tm == 0 and n % tn == 0 and ka % tk == 0
    spa = ka // tk
    nk = n_a * spa
    residual = x is not None

    def a_map(p):
        return lambda i, j, k: (i, jnp.clip(k - p * spa, 0, spa - 1))

    in_specs = [pl.BlockSpec((tm, tk), a_map(p)) for p in range(n_a)]
    in_specs.append(pl.BlockSpec((tk, tn), lambda i, j, k: (k, j)))
    args = list(a_list) + [w]
    if residual:
        in_specs += [pl.BlockSpec((tm, tn), lambda i, j, k: (i, j)),
                     pl.BlockSpec((1, tn), lambda i, j, k: (0, j))]
        args += [x, gate.reshape(1, n)]
        out_dtype = F32
    return pl.pallas_call(
        functools.partial(_matmul_kernel, n_a=n_a, steps_per_a=spa, residual=residual),
        grid=(m // tm, n // tn, nk),
        in_specs=in_specs,
        out_specs=pl.BlockSpec((tm, tn), lambda i, j, k: (i, j)),
        out_shape=jax.ShapeDtypeStruct((m, n), out_dtype),
        scratch_shapes=[pltpu.VMEM((tm, tn), F32)] if nk > 1 else [],
        compiler_params=_params("parallel", "parallel", "arbitrary"),
    )(*args)


def _hgrn_level_ids(r, reverse):
    t = np.arange(r)[:, None]
    s = np.arange(r)[None, :]
    ids = np.full((r, r), -1, np.int32)
    tri = (s >= t) if reverse else (s <= t)
    ids[(t // HG_LEAF == s // HG_LEAF) & tri] = 0
    lv, i = HG_LEAF, 1
    while 2 * lv <= r:
        same = (t // (2 * lv)) == (s // (2 * lv))
        t_hi, s_hi = (t & lv) != 0, (s & lv) != 0
        pair = (~t_hi & s_hi) if reverse else (t_hi & ~s_hi)
        ids[same & pair] = i
        lv, i = 2 * lv, i + 1
    return ids


def _hgrn_kernel(q_ref, z_ref, v_ref, lbl_ref, ids_ref, *rest, reverse, layer, heads, final):
    if final:
        ofwd_ref, gate_ref, ng_ref, o_ref, st_ref = rest
    else:
        o_ref, st_ref = rest
    r = q_ref.shape[0]
    hd = HG_HEAD_DIM

    @pl.when(pl.program_id(1) == 0)
    def _():
        st_ref[...] = jnp.zeros_like(st_ref)

    logits = lbl_ref[0]
    e = jnp.exp(logits - jnp.max(logits, axis=0, keepdims=True))
    lb_all = (jnp.sum(e[1:layer + 1], axis=0, keepdims=True) / jnp.sum(e, axis=0, keepdims=True)
              if layer > 0 else jnp.zeros_like(e[0:1]))

    ids = ids_ref[...]
    row = lax.broadcasted_iota(jnp.int32, (r, r), 0)
    col = lax.broadcasted_iota(jnp.int32, (r, r), 1)
    tri = jnp.where((col >= row) if reverse else (col <= row), 1.0, 0.0).astype(BF16)
    total_row = 0 if reverse else r - 1

    for h in range(heads):
        sl = slice(h * hd, (h + 1) * hd)
        lb = lb_all[:, sl]
        q = q_ref[:, sl].astype(F32) * (hd ** -0.5)
        z = z_ref[:, sl].astype(F32)
        v = v_ref[:, sl]
        sig = _sigmoid(z)
        g = jnp.log(lb + (1.0 - lb) * sig)
        kc = (1.0 - lb) * (1.0 - sig)

        g1 = g.astype(BF16)
        rem = g - g1.astype(F32)
        g2 = rem.astype(BF16)
        g3 = (rem - g2.astype(F32)).astype(BF16)
        b = _dot(tri, g1) + _dot(tri, g2) + _dot(tri, g3)

        a = jnp.zeros((r, r), F32)
        lv, level = HG_LEAF // 2, 0
        while 2 * lv <= r:
            blk = 2 * lv
            nb = r // blk
            ref_row = lv if reverse else lv - 1
            bb = b.reshape(nb, blk, hd)
            d = bb - bb[:, ref_row:ref_row + 1, :]
            cap = HG_LEAF_CLAMP if level == 0 else 0.0
            qe = (q.reshape(nb, blk, hd) * jnp.exp(jnp.minimum(d, cap))).reshape(r, hd).astype(BF16)
            ke = (kc.reshape(nb, blk, hd) * jnp.exp(jnp.minimum(-d, cap))).reshape(r, hd).astype(BF16)
            a = jnp.where(ids == level, _dot_nt(qe, ke), a)
            lv, level = 2 * lv, level + 1

        st = st_ref[h]
        o = _dot(a.astype(BF16), v) + _dot_nt((q * jnp.exp(b)).astype(BF16), st.astype(BF16))
        bl = b[total_row:total_row + 1, :]
        kd = (kc * jnp.exp(bl - b)).astype(BF16)
        st_ref[h] = jnp.exp(bl) * st + _dot_tn(v, kd)

        if final:
            o = o + ofwd_ref[:, sl]
            y = o * lax.rsqrt(jnp.mean(o * o, axis=-1, keepdims=True) + EPS) * ng_ref[...]
            gt = gate_ref[:, sl].astype(F32)
            o_ref[:, sl] = (y * (gt * _sigmoid(gt))).astype(o_ref.dtype)
        else:
            o_ref[:, sl] = o


def _hgrn_pass(proj, lb_logits, layer, d_hgrn, *, reverse, o_fwd=None, norm_g=None, chunk=256):
    t = proj.shape[0]
    heads_total = d_hgrn // HG_HEAD_DIM
    hb = _pick(heads_total, 4, 2, 1)
    w = hb * HG_HEAD_DIM
    r = min(chunk, t)
    assert t % r == 0 and r % (2 * HG_LEAF) == 0
    nchunks = t // r
    cpb = d_hgrn // w
    final = o_fwd is not None
    depth = lb_logits.shape[1]

    def rows(n):
        return (nchunks - 1 - n) if reverse else n

    def col_spec(group):
        return pl.BlockSpec((r, w), lambda h, n: (rows(n), group * cpb + h))

    in_specs = [col_spec(0), col_spec(2 if reverse else 1), col_spec(3),
                pl.BlockSpec((1, depth, w), lambda h, n: (1 if reverse else 0, 0, h)),
                pl.BlockSpec((r, r), lambda h, n: (0, 0))]
    args = [proj, proj, proj, lb_logits, jnp.asarray(_hgrn_level_ids(r, reverse))]
    if final:
        in_specs += [pl.BlockSpec((r, w), lambda h, n: (rows(n), h)), col_spec(4),
                     pl.BlockSpec((1, HG_HEAD_DIM), lambda h, n: (0, 0))]
        args += [o_fwd, proj, norm_g.reshape(1, HG_HEAD_DIM)]
    return pl.pallas_call(
        functools.partial(_hgrn_kernel, reverse=reverse, layer=layer, heads=hb, final=final),
        grid=(heads_total // hb, nchunks),
        in_specs=in_specs,
        out_specs=pl.BlockSpec((r, w), lambda h, n: (rows(n), h)),
        out_shape=jax.ShapeDtypeStruct((t, d_hgrn), BF16 if final else F32),
        scratch_shapes=[pltpu.VMEM((hb, HG_HEAD_DIM, HG_HEAD_DIM), F32)],
        compiler_params=_params("parallel", "arbitrary"),
    )(*args)


def _scaled_q(q_ref):
    return (q_ref[...].astype(F32) * (DA_HEAD_DIM ** -0.5 * LOG2E)).astype(BF16)


def _max_sq_norm(x):
    sq = x.astype(F32)
    sq = sq * sq
    n = jnp.maximum(jnp.sum(sq[:, :DA_HEAD_DIM], axis=-1, keepdims=True),
                    jnp.sum(sq[:, DA_HEAD_DIM:], axis=-1, keepdims=True))
    return jnp.max(n, axis=0, keepdims=True)


def _attn_window_kernel(q_ref, k_ref, slope_ref, o_ref, kmax_ref, *, tk, chunk):
    nk = k_ref.shape[0]

    @pl.when(pl.program_id(1) == 0)
    def _():
        def body(r, acc):
            rows = k_ref[pl.ds(pl.multiple_of(r * chunk, chunk), chunk), :]
            return jnp.maximum(acc, _max_sq_norm(rows))
        kmax = lax.fori_loop(0, nk // chunk, body, jnp.zeros((1, 1), F32))
        kmax_ref[...] = jnp.broadcast_to(kmax, kmax_ref.shape)

    bound = jnp.sqrt(_max_sq_norm(_scaled_q(q_ref)) * kmax_ref[0:1, 0:1])
    c = slope_ref[0][0:1, 0:1] * LOG2E
    w = jnp.floor((2.0 * bound + DA_SKIP_MARGIN) / (c * tk)) + 1.0
    w = jnp.minimum(w, float(nk // tk))
    o_ref[...] = jnp.broadcast_to(w, o_ref.shape).astype(jnp.int32)


def _diff_attn_kernel(win_ref, q_ref, k_ref, v_ref, slope_ref, lam_ref, ng_ref, o_ref,
                      qs_ref, bias_ref, m_ref, l_ref, acc_ref, *, lam_init, tk):
    h, i = pl.program_id(0), pl.program_id(1)
    tq = q_ref.shape[0]
    nkv = k_ref.shape[0] // tk
    hd = DA_HEAD_DIM
    c = slope_ref[0][0:1, 0:1] * LOG2E

    @pl.when(i == 0)
    def _():
        a = lax.broadcasted_iota(jnp.int32, (tq, tk), 0)
        b = lax.broadcasted_iota(jnp.int32, (tq, tk), 1)
        d = (a - b).astype(F32) * c
        bias_ref[0] = -d
        bias_ref[1] = d
        bias_ref[2] = -jnp.abs(d)

    qs_ref[...] = _scaled_q(q_ref)
    m_ref[...] = jnp.full_like(m_ref, -jnp.inf)
    l_ref[...] = jnp.zeros_like(l_ref)
    acc_ref[...] = jnp.zeros_like(acc_ref)

    w = win_ref[h, i]
    lo = jnp.maximum(i - w, 0)
    hi = jnp.minimum(i + w, nkv - 1) + 1

    def body(j, carry):
        off = pl.multiple_of(j * tk, tk)
        kt = k_ref[pl.ds(off, tk), :]
        vt = v_ref[pl.ds(off, tk), :]
        bias = bias_ref[jnp.where(j < i, 0, jnp.where(j > i, 1, 2))]
        cij = -c * jnp.abs(i * tq - j * tk).astype(F32)
        ps, alphas = [], []
        for mp in range(2):
            sl = slice(mp * hd, (mp + 1) * hd)
            s = _dot_nt(qs_ref[:, sl], kt[:, sl]) + bias
            m_old = m_ref[mp]
            m_new = jnp.maximum(m_old, jnp.max(s, axis=-1, keepdims=True) + cij)
            alpha = jnp.exp2(m_old - m_new)
            p = jnp.exp2(s - (m_new - cij))
            lsum = p[:, 0:128]
            for kb in range(1, tk // 128):
                lsum = lsum + p[:, kb * 128:(kb + 1) * 128]
            l_ref[mp] = alpha * l_ref[mp] + lsum
            m_ref[mp] = m_new
            ps.append(p.astype(BF16))
            alphas.append(alpha)
        pv = _dot(jnp.concatenate(ps, axis=0), vt)
        for mp in range(2):
            acc_ref[mp] = alphas[mp] * acc_ref[mp] + pv[mp * tq:(mp + 1) * tq]
        return carry

    lax.fori_loop(lo, hi, body, 0)

    lp = lam_ref[...]
    lam = (jnp.exp(jnp.sum(lp[0:1] * lp[1:2], axis=-1, keepdims=True))
           - jnp.exp(jnp.sum(lp[2:3] * lp[3:4], axis=-1, keepdims=True)) + lam_init)
    l0 = jnp.sum(l_ref[0], axis=-1, keepdims=True)
    l1 = jnp.sum(l_ref[1], axis=-1, keepdims=True)
    o = acc_ref[0] / l0 - lam * (acc_ref[1] / l1)
    y = o * lax.rsqrt(jnp.mean(o * o, axis=-1, keepdims=True) + EPS) * ng_ref[...]
    o_ref[...] = (y * (1.0 - lam_init)).astype(o_ref.dtype)


def _diff_attention(proj, lam_params, norm_g, layer, d_hgrn, d_diff, *, tile=512):
    t = proj.shape[0]
    heads = d_diff // DA_V_DIM
    tq = tk = min(tile, t)
    assert t % tq == 0 and tk % 128 == 0
    nq = t // tq
    lam_init = 0.8 - 0.6 * math.exp(-0.3 * layer)
    base = 5 * d_hgrn // DA_V_DIM
    slopes = np.exp2(-8.0 * np.arange(1, heads + 1, dtype=np.float32) / heads).astype(np.float32)
    slopes = jnp.asarray(np.broadcast_to(slopes[:, None, None], (heads, 1, 128)))

    win = pl.pallas_call(
        functools.partial(_attn_window_kernel, tk=tk, chunk=min(1024, t)),
        grid=(heads, nq),
        in_specs=[pl.BlockSpec((tq, DA_V_DIM), lambda h, i: (i, base + h)),
                  pl.BlockSpec((t, DA_V_DIM), lambda h, i: (0, base + heads + h)),
                  pl.BlockSpec((1, 1, 128), lambda h, i: (h, 0, 0))],
        out_specs=pl.BlockSpec((1, 1, 8, 128), lambda h, i: (h, i, 0, 0)),
        out_shape=jax.ShapeDtypeStruct((heads, nq, 8, 128), jnp.int32),
        scratch_shapes=[pltpu.VMEM((8, 128), F32)],
        compiler_params=_params("parallel", "arbitrary"),
    )(proj, proj, slopes)[:, :, 0, 0]

    grid_spec = pltpu.PrefetchScalarGridSpec(
        num_scalar_prefetch=1,
        grid=(heads, nq),
        in_specs=[pl.BlockSpec((tq, DA_V_DIM), lambda h, i, w: (i, base + h)),
                  pl.BlockSpec((t, DA_V_DIM), lambda h, i, w: (0, base + heads + h)),
                  pl.BlockSpec((t, DA_V_DIM), lambda h, i, w: (0, base + 2 * heads + h)),
                  pl.BlockSpec((1, 1, 128), lambda h, i, w: (h, 0, 0)),
                  pl.BlockSpec((4, DA_HEAD_DIM), lambda h, i, w: (0, 0)),
                  pl.BlockSpec((1, DA_V_DIM), lambda h, i, w: (0, 0))],
        out_specs=pl.BlockSpec((tq, DA_V_DIM), lambda h, i, w: (i, h)),
        scratch_shapes=[pltpu.VMEM((tq, DA_V_DIM), BF16),
                        pltpu.VMEM((3, tq, tk), F32),
                        pltpu.VMEM((2, tq, 1), F32),
                        pltpu.VMEM((2, tq, 128), F32),
                        pltpu.VMEM((2, tq, DA_V_DIM), F32)])
    return pl.pallas_call(
        functools.partial(_diff_attn_kernel, lam_init=lam_init, tk=tk),
        grid_spec=grid_spec,
        out_shape=jax.ShapeDtypeStruct((t, d_diff), BF16),
        compiler_params=_params("parallel", "arbitrary"),
    )(win, proj, proj, proj, slopes, lam_params, norm_g.reshape(1, DA_V_DIM))


def _conv_glu_kernel(ug_ref, uv_ref, pg_ref, pv_ref, ng_ref, nv_ref, cwg_ref, cwv_ref, cbg_ref, cbv_ref, o_ref):
    i = pl.program_id(0)
    tr = ug_ref.shape[0]
    has_prev = jnp.where(i > 0, 1.0, 0.0)
    has_next = jnp.where(i < pl.num_programs(0) - 1, 1.0, 0.0)
    row = lax.broadcasted_iota(jnp.int32, ug_ref.shape, 0)

    def conv(u_ref, p_ref, n_ref, cw_ref, cb_ref):
        u = u_ref[...].astype(F32)
        prev = p_ref[7:8, :].astype(F32) * has_prev
        nxt = n_ref[0:1, :].astype(F32) * has_next
        up = jnp.where(row == 0, prev, pltpu.roll(u, 1, axis=0))
        un = jnp.where(row == tr - 1, nxt, pltpu.roll(u, tr - 1, axis=0))
        cw = cw_ref[...]
        return up * cw[0:1] + u * cw[1:2] + un * cw[2:3] + cb_ref[...]

    gate = conv(ug_ref, pg_ref, ng_ref, cwg_ref, cbg_ref)
    val = conv(uv_ref, pv_ref, nv_ref, cwv_ref, cbv_ref)
    o_ref[...] = (gate * _sigmoid(gate) * val).astype(o_ref.dtype)


def _conv_glu(u, conv_w, conv_b, *, tr=256, tc=1024):
    t, n2 = u.shape
    dff = n2 // 2
    tr, tc = min(tr, t), min(tc, dff)
    assert t % tr == 0 and dff % tc == 0 and tr % 8 == 0
    ncb = dff // tc
    r8 = tr // 8
    last8 = t // 8 - 1

    def main(off):
        return pl.BlockSpec((tr, tc), lambda i, j: (i, off + j))

    def prev(off):
        return pl.BlockSpec((8, tc), lambda i, j: (jnp.maximum(i * r8 - 1, 0), off + j))

    def nxt(off):
        return pl.BlockSpec((8, tc), lambda i, j: (jnp.minimum((i + 1) * r8, last8), off + j))

    def prm(rows, off):
        return pl.BlockSpec((rows, tc), lambda i, j: (0, off + j))

    cb = conv_b.reshape(1, n2)
    return pl.pallas_call(
        _conv_glu_kernel,
        grid=(t // tr, ncb),
        in_specs=[main(0), main(ncb), prev(0), prev(ncb), nxt(0), nxt(ncb),
                  prm(CONV_WIDTH, 0), prm(CONV_WIDTH, ncb), prm(1, 0), prm(1, ncb)],
        out_specs=pl.BlockSpec((tr, tc), lambda i, j: (i, j)),
        out_shape=jax.ShapeDtypeStruct((t, dff), BF16),
        compiler_params=_params("parallel", "parallel"),
    )(u, u, u, u, u, u, conv_w, conv_w, cb, cb)


def kernel(x, c, w_ada, b_ada, ada_table, norm1_g, w_in, hg_lb_logits, hg_norm_g, da_lambda, da_norm_g,
           w_out, norm2_g, w_up, conv_w, conv_b, w_down, final_g):
    bsz, t, d = x.shape
    assert bsz == 1
    depth = w_in.shape[0]
    d_hgrn = hg_lb_logits.shape[-1]
    d_diff = w_out.shape[1] - d_hgrn
    assert w_in.shape[2] == 5 * d_hgrn + 3 * d_diff

    mod = _ada_mod(c, w_ada, b_ada, ada_table)
    xs = x.reshape(t, d)
    for l in range(depth):
        m = [mod[l, i * d:(i + 1) * d] for i in range(N_MOD)]
        h = _rmsnorm(xs, norm1_g[l], m[0], m[1])
        proj = _matmul([h], w_in[l].astype(BF16), tm=1024, tn=1024, tk=d)
        o_fwd = _hgrn_pass(proj, hg_lb_logits, l, d_hgrn, reverse=False)
        o_hg = _hgrn_pass(proj, hg_lb_logits, l, d_hgrn, reverse=True, o_fwd=o_fwd, norm_g=hg_norm_g[l])
        o_da = _diff_attention(proj, da_lambda[l], da_norm_g[l], l, d_hgrn, d_diff)
        if d_hgrn == d_diff:
            mix = [o_hg, o_da]
        else:
            mix = [jnp.concatenate([o_hg, o_da], axis=1)]
        xs = _matmul(mix, w_out[l].astype(BF16), tm=1024, tn=512, tk=2048, x=xs, gate=m[2])
        h = _rmsnorm(xs, norm2_g[l], m[3], m[4])
        u = _matmul([h], w_up[l].astype(BF16), tm=1024, tn=1024, tk=d)
        act = _conv_glu(u, conv_w[l], conv_b[l])
        xs = _matmul([act], w_down[l].astype(BF16), tm=1024, tn=1024, tk=2048, x=xs, gate=m[5])
    out = _rmsnorm(xs, final_g, out_dtype=F32)
    return out.reshape(bsz, t, d)
```

```python
import functools
import math

import jax
import jax.numpy as jnp
import numpy as np
from jax import lax
from jax.experimental import pallas as pl
from jax.experimental.pallas import tpu as pltpu

F32 = jnp.float32
BF16 = jnp.bfloat16

HG_HEAD_DIM = 128
DA_HEAD_DIM = 128
DA_V_DIM = 2 * DA_HEAD_DIM
N_MOD = 6
CONV_WIDTH = 3
EPS = 1e-6
HG_LEAF = 16
HG_LEAF_CLAMP = 80.0
DA_SKIP_MARGIN = 110.0
DA_FAST_RISE = 60.0
VMEM_LIMIT_BYTES = 56 * 1024 * 1024


def _params(*semantics):
    return pltpu.CompilerParams(dimension_semantics=semantics, vmem_limit_bytes=VMEM_LIMIT_BYTES)


def _dot(a, b):
    return lax.dot_general(a, b, (((1,), (0,)), ((), ())), preferred_element_type=F32)


def _dot_nt(a, b):
    return lax.dot_general(a, b, (((1,), (1,)), ((), ())), preferred_element_type=F32)


def _dot_tn(a, b):
    return lax.dot_general(a, b, (((0,), (0,)), ((), ())), preferred_element_type=F32)


def _sigmoid(x):
    return 1.0 / (1.0 + jnp.exp(-x))


def _pick(n, *cands):
    for c in cands:
        if n % c == 0:
            return c
    return n


def _mod_kernel(c_ref, w_ref, b_ref, t_ref, o_ref):
    c = c_ref[...]
    s = c * _sigmoid(c)
    y = jnp.dot(s, w_ref[...], preferred_element_type=F32, precision=lax.Precision.HIGHEST)
    o_ref[...] = y[0:1, :] + b_ref[...] + t_ref[...]


def _ada_mod(c, w_ada, b_ada, ada_table):
    _, d = c.shape
    depth = ada_table.shape[0]
    n = w_ada.shape[1]
    tn = _pick(n, 512, 256, 128)
    c8 = jnp.broadcast_to(c, (8, d))
    return pl.pallas_call(
        _mod_kernel,
        grid=(n // tn,),
        in_specs=[pl.BlockSpec((8, d), lambda j: (0, 0)),
                  pl.BlockSpec((d, tn), lambda j: (0, j)),
                  pl.BlockSpec((1, tn), lambda j: (0, j)),
                  pl.BlockSpec((depth, tn), lambda j: (0, j))],
        out_specs=pl.BlockSpec((depth, tn), lambda j: (0, j)),
        out_shape=jax.ShapeDtypeStruct((depth, n), F32),
        compiler_params=_params("parallel"),
    )(c8, w_ada, b_ada.reshape(1, n), ada_table.reshape(depth, n))


def _norm_kernel(x_ref, g_ref, *rest, modulated):
    if modulated:
        shift_ref, scale_ref, o_ref = rest
    else:
        (o_ref,) = rest
    x = x_ref[...]
    y = x * lax.rsqrt(jnp.mean(x * x, axis=-1, keepdims=True) + EPS) * g_ref[...]
    if modulated:
        y = y * (1.0 + scale_ref[...]) + shift_ref[...]
    o_ref[...] = y.astype(o_ref.dtype)


def _rmsnorm(x, g, shift=None, scale=None, out_dtype=BF16):
    t, d = x.shape
    tr = _pick(t, 256, 128, 64, 32, 16, 8)
    modulated = shift is not None
    row = pl.BlockSpec((1, d), lambda i: (0, 0))
    args = [x, g.reshape(1, d)] + ([shift.reshape(1, d), scale.reshape(1, d)] if modulated else [])
    return pl.pallas_call(
        functools.partial(_norm_kernel, modulated=modulated),
        grid=(t // tr,),
        in_specs=[pl.BlockSpec((tr, d), lambda i: (i, 0))] + [row] * (len(args) - 1),
        out_specs=pl.BlockSpec((tr, d), lambda i: (i, 0)),
        out_shape=jax.ShapeDtypeStruct((t, d), out_dtype),
        compiler_params=_params("parallel"),
    )(*args)


def _matmul_kernel(*refs, n_a, steps_per_a, residual):
    a_refs = refs[:n_a]
    w_ref = refs[n_a]
    pos = n_a + 1
    if residual:
        x_ref, gate_ref = refs[pos], refs[pos + 1]
        pos += 2
    o_ref = refs[pos]
    acc_ref = refs[pos + 1] if len(refs) > pos + 1 else None
    nk = n_a * steps_per_a
    k = pl.program_id(2)

    def finish(acc):
        if residual:
            o_ref[...] = x_ref[...] + gate_ref[...] * acc
        else:
            o_ref[...] = acc.astype(o_ref.dtype)

    if nk == 1:
        finish(_dot(a_refs[0][...], w_ref[...]))
        return

    for p in range(n_a):
        @pl.when(k // steps_per_a == p)
        def _(p=p):
            part = _dot(a_refs[p][...], w_ref[...])

            @pl.when(k == 0)
            def _():
                acc_ref[...] = part

            @pl.when(k > 0)
            def _():
                acc_ref[...] += part

    @pl.when(k == nk - 1)
    def _():
        finish(acc_ref[...])


def _matmul(a_list, w, *, tm, tn, tk, out_dtype=BF16, x=None, gate=None):
    m, ka = a_list[0].shape
    n_a = len(a_list)
    kt, n = w.shape
    assert kt == n_a * ka and all(a.shape == (m, ka) for a in a_list)
    tm, tn, tk = min(tm, m), min(tn, n), min(tk, ka)
    assert m % tm == 0 and n % tn == 0 and ka % tk == 0
    spa = ka // tk
    nk = n_a * spa
    residual = x is not None

    def a_map(p):
        return lambda i, j, k: (i, jnp.clip(k - p * spa, 0, spa - 1))

    in_specs = [pl.BlockSpec((tm, tk), a_map(p)) for p in range(n_a)]
    in_specs.append(pl.BlockSpec((tk, tn), lambda i, j, k: (k, j)))
    args = list(a_list) + [w]
    if residual:
        in_specs += [pl.BlockSpec((tm, tn), lambda i, j, k: (i, j)),
                     pl.BlockSpec((1, tn), lambda i, j, k: (0, j))]
        args += [x, gate.reshape(1, n)]
        out_dtype = F32
    return pl.pallas_call(
        functools.partial(_matmul_kernel, n_a=n_a, steps_per_a=spa, residual=residual),
        grid=(m // tm, n // tn, nk),
        in_specs=in_specs,
        out_specs=pl.BlockSpec((tm, tn), lambda i, j, k: (i, j)),
        out_shape=jax.ShapeDtypeStruct((m, n), out_dtype),
        scratch_shapes=[pltpu.VMEM((tm, tn), F32)] if nk > 1 else [],
        compiler_params=_params("parallel", "parallel", "arbitrary"),
    )(*args)


def _hgrn_level_ids(r, reverse):
    t = np.arange(r)[:, None]
    s = np.arange(r)[None, :]
    ids = np.full((r, r), -1, np.int32)
    tri = (s >= t) if reverse else (s <= t)
    ids[(t // HG_LEAF == s // HG_LEAF) & tri] = 0
    lv, i = HG_LEAF, 1
    while 2 * lv <= r:
        same = (t // (2 * lv)) == (s // (2 * lv))
        t_hi, s_hi = (t & lv) != 0, (s & lv) != 0
        pair = (~t_hi & s_hi) if reverse else (t_hi & ~s_hi)
        ids[same & pair] = i
        lv, i = 2 * lv, i + 1
    return ids


def _hgrn_kernel(q_ref, z_ref, v_ref, lbl_ref, ids_ref, *rest, reverse, layer, heads, final):
    if final:
        ofwd_ref, gate_ref, ng_ref, o_ref, st_ref = rest
    else:
        o_ref, st_ref = rest
    r = q_ref.shape[0]
    hd = HG_HEAD_DIM

    @pl.when(pl.program_id(1) == 0)
    def _():
        st_ref[...] = jnp.zeros_like(st_ref)

    logits = lbl_ref[0]
    e = jnp.exp(logits - jnp.max(logits, axis=0, keepdims=True))
    lb_all = (jnp.sum(e[1:layer + 1], axis=0, keepdims=True) / jnp.sum(e, axis=0, keepdims=True)
              if layer > 0 else jnp.zeros_like(e[0:1]))

    ids = ids_ref[...]
    row = lax.broadcasted_iota(jnp.int32, (r, r), 0)
    col = lax.broadcasted_iota(jnp.int32, (r, r), 1)
    tri = jnp.where((col >= row) if reverse else (col <= row), 1.0, 0.0).astype(BF16)
    total_row = 0 if reverse else r - 1

    for h in range(heads):
        sl = slice(h * hd, (h + 1) * hd)
        lb = lb_all[:, sl]
        q = q_ref[:, sl].astype(F32) * (hd ** -0.5)
        z = z_ref[:, sl].astype(F32)
        v = v_ref[:, sl]
        sig = _sigmoid(z)
        g = jnp.log(lb + (1.0 - lb) * sig)
        kc = (1.0 - lb) * (1.0 - sig)

        g1 = g.astype(BF16)
        rem = g - g1.astype(F32)
        g2 = rem.astype(BF16)
        g3 = (rem - g2.astype(F32)).astype(BF16)
        b = _dot(tri, g1) + _dot(tri, g2) + _dot(tri, g3)

        a = jnp.zeros((r, r), F32)
        lv, level = HG_LEAF // 2, 0
        while 2 * lv <= r:
            blk = 2 * lv
            nb = r // blk
            ref_row = lv if reverse else lv - 1
            bb = b.reshape(nb, blk, hd)
            d = bb - bb[:, ref_row:ref_row + 1, :]
            cap = HG_LEAF_CLAMP if level == 0 else 0.0
            qe = (q.reshape(nb, blk, hd) * jnp.exp(jnp.minimum(d, cap))).reshape(r, hd).astype(BF16)
            ke = (kc.reshape(nb, blk, hd) * jnp.exp(jnp.minimum(-d, cap))).reshape(r, hd).astype(BF16)
            a = jnp.where(ids == level, _dot_nt(qe, ke), a)
            lv, level = 2 * lv, level + 1

        st = st_ref[h]
        o = _dot(a.astype(BF16), v) + _dot_nt((q * jnp.exp(b)).astype(BF16), st.astype(BF16))
        bl = b[total_row:total_row + 1, :]
        kd = (kc * jnp.exp(bl - b)).astype(BF16)
        st_ref[h] = jnp.exp(bl) * st + _dot_tn(v, kd)

        if final:
            o = o + ofwd_ref[:, sl]
            y = o * lax.rsqrt(jnp.mean(o * o, axis=-1, keepdims=True) + EPS) * ng_ref[...]
            gt = gate_ref[:, sl].astype(F32)
            o_ref[:, sl] = (y * (gt * _sigmoid(gt))).astype(o_ref.dtype)
        else:
            o_ref[:, sl] = o


def _hgrn_pass(proj, lb_logits, layer, d_hgrn, *, reverse, o_fwd=None, norm_g=None, chunk=256):
    t = proj.shape[0]
    heads_total = d_hgrn // HG_HEAD_DIM
    hb = _pick(heads_total, 4, 2, 1)
    w = hb * HG_HEAD_DIM
    r = min(chunk, t)
    assert t % r == 0 and r % (2 * HG_LEAF) == 0
    nchunks = t // r
    cpb = d_hgrn // w
    final = o_fwd is not None
    depth = lb_logits.shape[1]

    def rows(n):
        return (nchunks - 1 - n) if reverse else n

    def col_spec(group):
        return pl.BlockSpec((r, w), lambda h, n: (rows(n), group * cpb + h))

    in_specs = [col_spec(0), col_spec(2 if reverse else 1), col_spec(3),
                pl.BlockSpec((1, depth, w), lambda h, n: (1 if reverse else 0, 0, h)),
                pl.BlockSpec((r, r), lambda h, n: (0, 0))]
    args = [proj, proj, proj, lb_logits, jnp.asarray(_hgrn_level_ids(r, reverse))]
    if final:
        in_specs += [pl.BlockSpec((r, w), lambda h, n: (rows(n), h)), col_spec(4),
                     pl.BlockSpec((1, HG_HEAD_DIM), lambda h, n: (0, 0))]
        args += [o_fwd, proj, norm_g.reshape(1, HG_HEAD_DIM)]
    return pl.pallas_call(
        functools.partial(_hgrn_kernel, reverse=reverse, layer=layer, heads=hb, final=final),
        grid=(heads_total // hb, nchunks),
        in_specs=in_specs,
        out_specs=pl.BlockSpec((r, w), lambda h, n: (rows(n), h)),
        out_shape=jax.ShapeDtypeStruct((t, d_hgrn), BF16 if final else F32),
        scratch_shapes=[pltpu.VMEM((hb, HG_HEAD_DIM, HG_HEAD_DIM), F32)],
        compiler_params=_params("parallel", "arbitrary"),
    )(*args)


def _scaled_q(q_ref):
    return (q_ref[...].astype(F32) * (DA_HEAD_DIM ** -0.5)).astype(BF16)


def _max_sq_norm(x):
    sq = x.astype(F32)
    sq = sq * sq
    n = jnp.maximum(jnp.sum(sq[:, :DA_HEAD_DIM], axis=-1, keepdims=True),
                    jnp.sum(sq[:, DA_HEAD_DIM:], axis=-1, keepdims=True))
    return jnp.max(n, axis=0, keepdims=True)


def _attn_window_kernel(q_ref, k_ref, slope_ref, win_ref, fast_ref, kmax_ref, *, tk, chunk):
    nk = k_ref.shape[0]

    @pl.when(pl.program_id(1) == 0)
    def _():
        def body(r, acc):
            rows = k_ref[pl.ds(pl.multiple_of(r * chunk, chunk), chunk), :]
            return jnp.maximum(acc, _max_sq_norm(rows))
        kmax = lax.fori_loop(0, nk // chunk, body, jnp.zeros((1, 1), F32))
        kmax_ref[...] = jnp.broadcast_to(kmax, kmax_ref.shape)

    bound = jnp.sqrt(_max_sq_norm(_scaled_q(q_ref)) * kmax_ref[0:1, 0:1])
    c = slope_ref[0][0:1, 0:1]
    w = jnp.floor((2.0 * bound + DA_SKIP_MARGIN) / (c * tk)) + 1.0
    w = jnp.minimum(w, float(nk // tk))
    win_ref[...] = jnp.broadcast_to(w, win_ref.shape).astype(jnp.int32)
    fast = jnp.where(2.0 * bound < DA_FAST_RISE, 1.0, 0.0)
    fast_ref[...] = jnp.broadcast_to(fast, fast_ref.shape).astype(jnp.int32)


def _lane_block_sum(p):
    out = p[:, 0:128]
    for kb in range(1, p.shape[1] // 128):
        out = out + p[:, kb * 128:(kb + 1) * 128]
    return out


def _diff_attn_kernel(win_ref, fast_ref, q_ref, k_ref, v_ref, slope_ref, lam_ref, ng_ref, o_ref,
                      qs_ref, bias_ref, m_ref, l_ref, acc_ref, *, lam_init, tk):
    h, i = pl.program_id(0), pl.program_id(1)
    tq = q_ref.shape[0]
    nkv = k_ref.shape[0] // tk
    hd = DA_HEAD_DIM
    c = slope_ref[0][0:1, 0:1]

    @pl.when(i == 0)
    def _():
        a = lax.broadcasted_iota(jnp.int32, (tq, tk), 0)
        b = lax.broadcasted_iota(jnp.int32, (tq, tk), 1)
        d = (a - b).astype(F32) * c
        bias_ref[0] = -d
        bias_ref[1] = d
        bias_ref[2] = -jnp.abs(d)

    qs_ref[...] = _scaled_q(q_ref)

    def tile(j):
        off = pl.multiple_of(j * tk, tk)
        cij = -c * jnp.abs(i * tq - j * tk).astype(F32)
        return k_ref[pl.ds(off, tk), :], v_ref[pl.ds(off, tk), :], cij

    def scores(mp, kt, bias):
        sl = slice(mp * hd, (mp + 1) * hd)
        return _dot_nt(qs_ref[:, sl], kt[:, sl]) + bias

    kt, vt, _ = tile(i)
    ps = []
    for mp in range(2):
        s = scores(mp, kt, bias_ref[2])
        m = jnp.max(s, axis=-1, keepdims=True)
        p = jnp.exp(s - m)
        m_ref[mp] = m
        l_ref[mp] = _lane_block_sum(p)
        ps.append(p.astype(BF16))
    acc_ref[...] = _dot(jnp.concatenate(ps, axis=0), vt)

    w = win_ref[h, i]
    lo = jnp.maximum(i - w, 0)
    n_other = jnp.minimum(i + w, nkv - 1) - lo
    fast = fast_ref[h, i]

    def other_tile(t):
        j = lo + t
        j = j + jnp.where(j >= i, 1, 0)
        return tile(j) + (bias_ref[jnp.where(j < i, 0, 1)],)

    def fast_body(t, carry):
        kt, vt, cij, bias = other_tile(t)
        ps = []
        for mp in range(2):
            p = jnp.exp(scores(mp, kt, bias) - (m_ref[mp] - cij))
            l_ref[mp] += _lane_block_sum(p)
            ps.append(p.astype(BF16))
        acc_ref[...] += _dot(jnp.concatenate(ps, axis=0), vt)
        return carry

    def safe_body(t, carry):
        kt, vt, cij, bias = other_tile(t)
        ps, alphas = [], []
        for mp in range(2):
            s = scores(mp, kt, bias)
            m_old = m_ref[mp]
            m_new = jnp.maximum(m_old, jnp.max(s, axis=-1, keepdims=True) + cij)
            alpha = jnp.exp(m_old - m_new)
            p = jnp.exp(s - (m_new - cij))
            l_ref[mp] = alpha * l_ref[mp] + _lane_block_sum(p)
            m_ref[mp] = m_new
            ps.append(p.astype(BF16))
            alphas.append(alpha)
        pv = _dot(jnp.concatenate(ps, axis=0), vt)
        for mp in range(2):
            rows = slice(mp * tq, (mp + 1) * tq)
            acc_ref[rows, :] = alphas[mp] * acc_ref[rows, :] + pv[rows]
        return carry

    lax.fori_loop(0, jnp.where(fast > 0, n_other, 0), fast_body, 0)
    lax.fori_loop(0, jnp.where(fast > 0, 0, n_other), safe_body, 0)

    lp = lam_ref[...]
    lam = (jnp.exp(jnp.sum(lp[0:1] * lp[1:2], axis=-1, keepdims=True))
           - jnp.exp(jnp.sum(lp[2:3] * lp[3:4], axis=-1, keepdims=True)) + lam_init)
    l0 = jnp.sum(l_ref[0], axis=-1, keepdims=True)
    l1 = jnp.sum(l_ref[1], axis=-1, keepdims=True)
    o = acc_ref[0:tq, :] / l0 - lam * (acc_ref[tq:2 * tq, :] / l1)
    y = o * lax.rsqrt(jnp.mean(o * o, axis=-1, keepdims=True) + EPS) * ng_ref[...]
    o_ref[...] = (y * (1.0 - lam_init)).astype(o_ref.dtype)


def _diff_attention(proj, lam_params, norm_g, layer, d_hgrn, d_diff, *, tile=512):
    t = proj.shape[0]
    heads = d_diff // DA_V_DIM
    tq = tk = min(tile, t)
    assert t % tq == 0 and tk % 128 == 0
    nq = t // tq
    lam_init = 0.8 - 0.6 * math.exp(-0.3 * layer)
    base = 5 * d_hgrn // DA_V_DIM
    slopes = np.exp2(-8.0 * np.arange(1, heads + 1, dtype=np.float32) / heads).astype(np.float32)
    slopes = jnp.asarray(np.broadcast_to(slopes[:, None, None], (heads, 1, 128)))

    flag_spec = pl.BlockSpec((1, 1, 8, 128), lambda h, i: (h, i, 0, 0))
    flag_shape = jax.ShapeDtypeStruct((heads, nq, 8, 128), jnp.int32)
    win, fast = pl.pallas_call(
        functools.partial(_attn_window_kernel, tk=tk, chunk=min(1024, t)),
        grid=(heads, nq),
        in_specs=[pl.BlockSpec((tq, DA_V_DIM), lambda h, i: (i, base + h)),
                  pl.BlockSpec((t, DA_V_DIM), lambda h, i: (0, base + heads + h)),
                  pl.BlockSpec((1, 1, 128), lambda h, i: (h, 0, 0))],
        out_specs=[flag_spec, flag_spec],
        out_shape=[flag_shape, flag_shape],
        scratch_shapes=[pltpu.VMEM((8, 128), F32)],
        compiler_params=_params("parallel", "arbitrary"),
    )(proj, proj, slopes)

    grid_spec = pltpu.PrefetchScalarGridSpec(
        num_scalar_prefetch=2,
        grid=(heads, nq),
        in_specs=[pl.BlockSpec((tq, DA_V_DIM), lambda h, i, w, f: (i, base + h)),
                  pl.BlockSpec((t, DA_V_DIM), lambda h, i, w, f: (0, base + heads + h)),
                  pl.BlockSpec((t, DA_V_DIM), lambda h, i, w, f: (0, base + 2 * heads + h)),
                  pl.BlockSpec((1, 1, 128), lambda h, i, w, f: (h, 0, 0)),
                  pl.BlockSpec((4, DA_HEAD_DIM), lambda h, i, w, f: (0, 0)),
                  pl.BlockSpec((1, DA_V_DIM), lambda h, i, w, f: (0, 0))],
        out_specs=pl.BlockSpec((tq, DA_V_DIM), lambda h, i, w, f: (i, h)),
        scratch_shapes=[pltpu.VMEM((tq, DA_V_DIM), BF16),
                        pltpu.VMEM((3, tq, tk), F32),
                        pltpu.VMEM((2, tq, 1), F32),
                        pltpu.VMEM((2, tq, 128), F32),
                        pltpu.VMEM((2 * tq, DA_V_DIM), F32)])
    return pl.pallas_call(
        functools.partial(_diff_attn_kernel, lam_init=lam_init, tk=tk),
        grid_spec=grid_spec,
        out_shape=jax.ShapeDtypeStruct((t, d_diff), BF16),
        compiler_params=_params("parallel", "arbitrary"),
    )(win[:, :, 0, 0], fast[:, :, 0, 0], proj, proj, proj, slopes, lam_params,
      norm_g.reshape(1, DA_V_DIM))


def _conv_glu_kernel(ug_ref, uv_ref, pg_ref, pv_ref, ng_ref, nv_ref, cwg_ref, cwv_ref, cbg_ref, cbv_ref, o_ref):
    i = pl.program_id(0)
    tr = ug_ref.shape[0]
    has_prev = jnp.where(i > 0, 1.0, 0.0)
    has_next = jnp.where(i < pl.num_programs(0) - 1, 1.0, 0.0)
    row = lax.broadcasted_iota(jnp.int32, ug_ref.shape, 0)

    def conv(u_ref, p_ref, n_ref, cw_ref, cb_ref):
        u = u_ref[...].astype(F32)
        prev = p_ref[7:8, :].astype(F32) * has_prev
        nxt = n_ref[0:1, :].astype(F32) * has_next
        up = jnp.where(row == 0, prev, pltpu.roll(u, 1, axis=0))
        un = jnp.where(row == tr - 1, nxt, pltpu.roll(u, tr - 1, axis=0))
        cw = cw_ref[...]
        return up * cw[0:1] + u * cw[1:2] + un * cw[2:3] + cb_ref[...]

    gate = conv(ug_ref, pg_ref, ng_ref, cwg_ref, cbg_ref)
    val = conv(uv_ref, pv_ref, nv_ref, cwv_ref, cbv_ref)
    o_ref[...] = (gate * _sigmoid(gate) * val).astype(o_ref.dtype)


def _conv_glu(u, conv_w, conv_b, *, tr=256, tc=1024):
    t, n2 = u.shape
    dff = n2 // 2
    tr, tc = min(tr, t), min(tc, dff)
    assert t % tr == 0 and dff % tc == 0 and tr % 8 == 0
    ncb = dff // tc
    r8 = tr // 8
    last8 = t // 8 - 1

    def main(off):
        return pl.BlockSpec((tr, tc), lambda i, j: (i, off + j))

    def prev(off):
        return pl.BlockSpec((8, tc), lambda i, j: (jnp.maximum(i * r8 - 1, 0), off + j))

    def nxt(off):
        return pl.BlockSpec((8, tc), lambda i, j: (jnp.minimum((i + 1) * r8, last8), off + j))

    def prm(rows, off):
        return pl.BlockSpec((rows, tc), lambda i, j: (0, off + j))

    cb = conv_b.reshape(1, n2)
    return pl.pallas_call(
        _conv_glu_kernel,
        grid=(t // tr, ncb),
        in_specs=[main(0), main(ncb), prev(0), prev(ncb), nxt(0), nxt(ncb),
                  prm(CONV_WIDTH, 0), prm(CONV_WIDTH, ncb), prm(1, 0), prm(1, ncb)],
        out_specs=pl.BlockSpec((tr, tc), lambda i, j: (i, j)),
        out_shape=jax.ShapeDtypeStruct((t, dff), BF16),
        compiler_params=_params("parallel", "parallel"),
    )(u, u, u, u, u, u, conv_w, conv_w, cb, cb)


def kernel(x, c, w_ada, b_ada, ada_table, norm1_g, w_in, hg_lb_logits, hg_norm_g, da_lambda, da_norm_g,
           w_out, norm2_g, w_up, conv_w, conv_b, w_down, final_g):
    bsz, t, d = x.shape
    assert bsz == 1
    depth = w_in.shape[0]
    d_hgrn = hg_lb_logits.shape[-1]
    d_diff = w_out.shape[1] - d_hgrn
    assert w_in.shape[2] == 5 * d_hgrn + 3 * d_diff

    mod = _ada_mod(c, w_ada, b_ada, ada_table)
    xs = x.reshape(t, d)
    for l in range(depth):
        m = [mod[l, i * d:(i + 1) * d] for i in range(N_MOD)]
        h = _rmsnorm(xs, norm1_g[l], m[0], m[1])
        proj = _matmul([h], w_in[l].astype(BF16), tm=1024, tn=1024, tk=d)
        o_fwd = _hgrn_pass(proj, hg_lb_logits, l, d_hgrn, reverse=False)
        o_hg = _hgrn_pass(proj, hg_lb_logits, l, d_hgrn, reverse=True, o_fwd=o_fwd, norm_g=hg_norm_g[l])
        o_da = _diff_attention(proj, da_lambda[l], da_norm_g[l], l, d_hgrn, d_diff)
        if d_hgrn == d_diff:
            mix = [o_hg, o_da]
        else:
            mix = [jnp.concatenate([o_hg, o_da], axis=1)]
        xs = _matmul(mix, w_out[l].astype(BF16), tm=1024, tn=512, tk=2048, x=xs, gate=m[2])
        h = _rmsnorm(xs, norm2_g[l], m[3], m[4])
        u = _matmul([h], w_up[l].astype(BF16), tm=1024, tn=1024, tk=d)
        act = _conv_glu(u, conv_w[l], conv_b[l])
        xs = _matmul([act], w_down[l].astype(BF16), tm=1024, tn=1024, tk=2048, x=xs, gate=m[5])
    out = _rmsnorm(xs, final_g, out_dtype=F32)
    return out.reshape(bsz, t, d)
```

```python
import functools
import math

import jax
import jax.numpy as jnp
import numpy as np
from jax import lax
from jax.experimental import pallas as pl
from jax.experimental.pallas import tpu as pltpu

F32 = jnp.float32
BF16 = jnp.bfloat16

HG_HEAD_DIM = 128
DA_HEAD_DIM = 128
DA_V_DIM = 2 * DA_HEAD_DIM
N_MOD = 6
CONV_WIDTH = 3
EPS = 1e-6
HG_LEAF = 16
HG_LEAF_CLAMP = 80.0
DA_SKIP_MARGIN = 110.0
DA_FAST_RISE = 60.0
VMEM_LIMIT_BYTES = 56 * 1024 * 1024


def _params(*semantics):
    return pltpu.CompilerParams(dimension_semantics=semantics, vmem_limit_bytes=VMEM_LIMIT_BYTES)


def _dot(a, b):
    return lax.dot_general(a, b, (((1,), (0,)), ((), ())), preferred_element_type=F32)


def _dot_nt(a, b):
    return lax.dot_general(a, b, (((1,), (1,)), ((), ())), preferred_element_type=F32)


def _dot_tn(a, b):
    return lax.dot_general(a, b, (((0,), (0,)), ((), ())), preferred_element_type=F32)


def _sigmoid(x):
    return 1.0 / (1.0 + jnp.exp(-x))


def _pick(n, *cands):
    for c in cands:
        if n % c == 0:
            return c
    return n


def _mod_kernel(c_ref, w_ref, b_ref, t_ref, o_ref):
    c = c_ref[...]
    s = c * _sigmoid(c)
    y = jnp.dot(s, w_ref[...], preferred_element_type=F32, precision=lax.Precision.HIGHEST)
    o_ref[...] = y[0:1, :] + b_ref[...] + t_ref[...]


def _ada_mod(c, w_ada, b_ada, ada_table):
    _, d = c.shape
    depth = ada_table.shape[0]
    n = w_ada.shape[1]
    tn = _pick(n, 512, 256, 128)
    c8 = jnp.broadcast_to(c, (8, d))
    return pl.pallas_call(
        _mod_kernel,
        grid=(n // tn,),
        in_specs=[pl.BlockSpec((8, d), lambda j: (0, 0)),
                  pl.BlockSpec((d, tn), lambda j: (0, j)),
                  pl.BlockSpec((1, tn), lambda j: (0, j)),
                  pl.BlockSpec((depth, tn), lambda j: (0, j))],
        out_specs=pl.BlockSpec((depth, tn), lambda j: (0, j)),
        out_shape=jax.ShapeDtypeStruct((depth, n), F32),
        compiler_params=_params("parallel"),
    )(c8, w_ada, b_ada.reshape(1, n), ada_table.reshape(depth, n))


def _norm_kernel(x_ref, g_ref, *rest, modulated):
    if modulated:
        shift_ref, scale_ref, o_ref = rest
    else:
        (o_ref,) = rest
    x = x_ref[...]
    y = x * lax.rsqrt(jnp.mean(x * x, axis=-1, keepdims=True) + EPS) * g_ref[...]
    if modulated:
        y = y * (1.0 + scale_ref[...]) + shift_ref[...]
    o_ref[...] = y.astype(o_ref.dtype)


def _rmsnorm(x, g, shift=None, scale=None, out_dtype=BF16):
    t, d = x.shape
    tr = _pick(t, 256, 128, 64, 32, 16, 8)
    modulated = shift is not None
    row = pl.BlockSpec((1, d), lambda i: (0, 0))
    args = [x, g.reshape(1, d)] + ([shift.reshape(1, d), scale.reshape(1, d)] if modulated else [])
    return pl.pallas_call(
        functools.partial(_norm_kernel, modulated=modulated),
        grid=(t // tr,),
        in_specs=[pl.BlockSpec((tr, d), lambda i: (i, 0))] + [row] * (len(args) - 1),
        out_specs=pl.BlockSpec((tr, d), lambda i: (i, 0)),
        out_shape=jax.ShapeDtypeStruct((t, d), out_dtype),
        compiler_params=_params("parallel"),
    )(*args)


def _matmul_kernel(*refs, n_a, steps_per_a, residual):
    a_refs = refs[:n_a]
    w_ref = refs[n_a]
    pos = n_a + 1
    if residual:
        x_ref, gate_ref = refs[pos], refs[pos + 1]
        pos += 2
    o_ref = refs[pos]
    acc_ref = refs[pos + 1] if len(refs) > pos + 1 else None
    nk = n_a * steps_per_a
    k = pl.program_id(2)

    def finish(acc):
        if residual:
            o_ref[...] = x_ref[...] + gate_ref[...] * acc
        else:
            o_ref[...] = acc.astype(o_ref.dtype)

    if nk == 1:
        finish(_dot(a_refs[0][...], w_ref[...]))
        return

    for p in range(n_a):
        @pl.when(k // steps_per_a == p)
        def _(p=p):
            part = _dot(a_refs[p][...], w_ref[...])

            @pl.when(k == 0)
            def _():
                acc_ref[...] = part

            @pl.when(k > 0)
            def _():
                acc_ref[...] += part

    @pl.when(k == nk - 1)
    def _():
        finish(acc_ref[...])


def _matmul(a_list, w, layer, *, tm, tn, tk, out_dtype=BF16, x=None, gate=None):
    m, ka = a_list[0].shape
    n_a = len(a_list)
    _, kt, n = w.shape
    assert kt == n_a * ka and all(a.shape == (m, ka) for a in a_list)
    tm, tn, tk = min(tm, m), min(tn, n), min(tk, ka)
    assert m % tm == 0 and n % tn == 0 and ka % tk == 0
    spa = ka // tk
    nk = n_a * spa
    residual = x is not None

    def a_map(p):
        return lambda i, j, k: (i, jnp.clip(k - p * spa, 0, spa - 1))

    in_specs = [pl.BlockSpec((tm, tk), a_map(p)) for p in range(n_a)]
    in_specs.append(pl.BlockSpec((None, tk, tn), lambda i, j, k: (layer, k, j)))
    args = list(a_list) + [w]
    if residual:
        in_specs += [pl.BlockSpec((tm, tn), lambda i, j, k: (i, j)),
                     pl.BlockSpec((1, tn), lambda i, j, k: (0, j))]
        args += [x, gate.reshape(1, n)]
        out_dtype = F32
    return pl.pallas_call(
        functools.partial(_matmul_kernel, n_a=n_a, steps_per_a=spa, residual=residual),
        grid=(m // tm, n // tn, nk),
        in_specs=in_specs,
        out_specs=pl.BlockSpec((tm, tn), lambda i, j, k: (i, j)),
        out_shape=jax.ShapeDtypeStruct((m, n), out_dtype),
        scratch_shapes=[pltpu.VMEM((tm, tn), F32)] if nk > 1 else [],
        compiler_params=_params("parallel", "parallel", "arbitrary"),
    )(*args)


def _hgrn_level_ids(r, reverse):
    t = np.arange(r)[:, None]
    s = np.arange(r)[None, :]
    ids = np.full((r, r), -1, np.int32)
    tri = (s >= t) if reverse else (s <= t)
    ids[(t // HG_LEAF == s // HG_LEAF) & tri] = 0
    lv, i = HG_LEAF, 1
    while 2 * lv <= r:
        same = (t // (2 * lv)) == (s // (2 * lv))
        t_hi, s_hi = (t & lv) != 0, (s & lv) != 0
        pair = (~t_hi & s_hi) if reverse else (t_hi & ~s_hi)
        ids[same & pair] = i
        lv, i = 2 * lv, i + 1
    return ids


def _hgrn_kernel(q_ref, z_ref, v_ref, lbl_ref, ids_ref, *rest, reverse, layer, heads, final):
    if final:
        ofwd_ref, gate_ref, ng_ref, o_ref, st_ref = rest
    else:
        o_ref, st_ref = rest
    r = q_ref.shape[0]
    hd = HG_HEAD_DIM

    @pl.when(pl.program_id(1) == 0)
    def _():
        st_ref[...] = jnp.zeros_like(st_ref)

    logits = lbl_ref[0]
    e = jnp.exp(logits - jnp.max(logits, axis=0, keepdims=True))
    lb_all = (jnp.sum(e[1:layer + 1], axis=0, keepdims=True) / jnp.sum(e, axis=0, keepdims=True)
              if layer > 0 else jnp.zeros_like(e[0:1]))

    ids = ids_ref[...]
    row = lax.broadcasted_iota(jnp.int32, (r, r), 0)
    col = lax.broadcasted_iota(jnp.int32, (r, r), 1)
    tri = jnp.where((col >= row) if reverse else (col <= row), 1.0, 0.0).astype(BF16)
    total_row = 0 if reverse else r - 1

    for h in range(heads):
        sl = slice(h * hd, (h + 1) * hd)
        lb = lb_all[:, sl]
        q = q_ref[:, sl].astype(F32) * (hd ** -0.5)
        z = z_ref[:, sl].astype(F32)
        v = v_ref[:, sl]
        sig = _sigmoid(z)
        g = jnp.log(lb + (1.0 - lb) * sig)
        kc = (1.0 - lb) * (1.0 - sig)

        g1 = g.astype(BF16)
        rem = g - g1.astype(F32)
        g2 = rem.astype(BF16)
        g3 = (rem - g2.astype(F32)).astype(BF16)
        b = _dot(tri, g1) + _dot(tri, g2) + _dot(tri, g3)

        a = jnp.zeros((r, r), F32)
        lv, level = HG_LEAF // 2, 0
        while 2 * lv <= r:
            blk = 2 * lv
            nb = r // blk
            ref_row = lv if reverse else lv - 1
            bb = b.reshape(nb, blk, hd)
            d = bb - bb[:, ref_row:ref_row + 1, :]
            cap = HG_LEAF_CLAMP if level == 0 else 0.0
            qe = (q.reshape(nb, blk, hd) * jnp.exp(jnp.minimum(d, cap))).reshape(r, hd).astype(BF16)
            ke = (kc.reshape(nb, blk, hd) * jnp.exp(jnp.minimum(-d, cap))).reshape(r, hd).astype(BF16)
            a = jnp.where(ids == level, _dot_nt(qe, ke), a)
            lv, level = 2 * lv, level + 1

        st = st_ref[h]
        o = _dot(a.astype(BF16), v) + _dot_nt((q * jnp.exp(b)).astype(BF16), st.astype(BF16))
        bl = b[total_row:total_row + 1, :]
        kd = (kc * jnp.exp(bl - b)).astype(BF16)
        st_ref[h] = jnp.exp(bl) * st + _dot_tn(v, kd)

        if final:
            o = o + ofwd_ref[:, sl]
            y = o * lax.rsqrt(jnp.mean(o * o, axis=-1, keepdims=True) + EPS) * ng_ref[...]
            gt = gate_ref[:, sl].astype(F32)
            o_ref[:, sl] = (y * (gt * _sigmoid(gt))).astype(o_ref.dtype)
        else:
            o_ref[:, sl] = o


def _hgrn_pass(proj, lb_logits, layer, d_hgrn, *, reverse, o_fwd=None, norm_g=None, chunk=256):
    t = proj.shape[0]
    heads_total = d_hgrn // HG_HEAD_DIM
    hb = _pick(heads_total, 4, 2, 1)
    w = hb * HG_HEAD_DIM
    r = min(chunk, t)
    assert t % r == 0 and r % (2 * HG_LEAF) == 0
    nchunks = t // r
    cpb = d_hgrn // w
    final = o_fwd is not None
    depth = lb_logits.shape[1]

    def rows(n):
        return (nchunks - 1 - n) if reverse else n

    def col_spec(group):
        return pl.BlockSpec((r, w), lambda h, n: (rows(n), group * cpb + h))

    in_specs = [col_spec(0), col_spec(2 if reverse else 1), col_spec(3),
                pl.BlockSpec((1, depth, w), lambda h, n: (1 if reverse else 0, 0, h)),
                pl.BlockSpec((r, r), lambda h, n: (0, 0))]
    args = [proj, proj, proj, lb_logits, jnp.asarray(_hgrn_level_ids(r, reverse))]
    if final:
        in_specs += [pl.BlockSpec((r, w), lambda h, n: (rows(n), h)), col_spec(4),
                     pl.BlockSpec((1, HG_HEAD_DIM), lambda h, n: (0, 0))]
        args += [o_fwd, proj, norm_g.reshape(1, HG_HEAD_DIM)]
    return pl.pallas_call(
        functools.partial(_hgrn_kernel, reverse=reverse, layer=layer, heads=hb, final=final),
        grid=(heads_total // hb, nchunks),
        in_specs=in_specs,
        out_specs=pl.BlockSpec((r, w), lambda h, n: (rows(n), h)),
        out_shape=jax.ShapeDtypeStruct((t, d_hgrn), BF16 if final else F32),
        scratch_shapes=[pltpu.VMEM((hb, HG_HEAD_DIM, HG_HEAD_DIM), F32)],
        compiler_params=_params("parallel", "arbitrary"),
    )(*args)


def _scaled_q(q_ref):
    return (q_ref[...].astype(F32) * (DA_HEAD_DIM ** -0.5)).astype(BF16)


def _max_sq_norm(x):
    sq = x.astype(F32)
    sq = sq * sq
    n = jnp.maximum(jnp.sum(sq[:, :DA_HEAD_DIM], axis=-1, keepdims=True),
                    jnp.sum(sq[:, DA_HEAD_DIM:], axis=-1, keepdims=True))
    return jnp.max(n, axis=0, keepdims=True)


def _lane_block_sum(p):
    out = p[:, 0:128]
    for kb in range(1, p.shape[1] // 128):
        out = out + p[:, kb * 128:(kb + 1) * 128]
    return out


def _diff_attn_kernel(q_ref, k_ref, v_ref, slope_ref, lam_ref, ng_ref, o_ref,
                      qs_ref, bias_ref, kmax_ref, m_ref, l_ref, acc_ref, *, lam_init, tk, chunk):
    i = pl.program_id(1)
    tq = q_ref.shape[0]
    nkv = k_ref.shape[0] // tk
    hd = DA_HEAD_DIM
    c = slope_ref[0][0:1, 0:1]

    @pl.when(i == 0)
    def _():
        a = lax.broadcasted_iota(jnp.int32, (tq, tk), 0)
        b = lax.broadcasted_iota(jnp.int32, (tq, tk), 1)
        d = (a - b).astype(F32) * c
        bias_ref[0] = -d
        bias_ref[1] = d
        bias_ref[2] = -jnp.abs(d)

        def body(r, acc):
            rows = k_ref[pl.ds(pl.multiple_of(r * chunk, chunk), chunk), :]
            return jnp.maximum(acc, _max_sq_norm(rows))
        kmax = lax.fori_loop(0, k_ref.shape[0] // chunk, body, jnp.zeros((1, 1), F32))
        kmax_ref[...] = jnp.broadcast_to(kmax, kmax_ref.shape)

    qs = _scaled_q(q_ref)
    qs_ref[...] = qs
    bound = jnp.sqrt(_max_sq_norm(qs) * kmax_ref[0:1, 0:1])

    def tile(j):
        off = pl.multiple_of(j * tk, tk)
        cij = -c * jnp.abs(i * tq - j * tk).astype(F32)
        return k_ref[pl.ds(off, tk), :], v_ref[pl.ds(off, tk), :], cij

    def scores(mp, kt, bias):
        sl = slice(mp * hd, (mp + 1) * hd)
        return _dot_nt(qs_ref[:, sl], kt[:, sl]) + bias

    kt, vt, _ = tile(i)
    ps, m_min = [], None
    for mp in range(2):
        s = scores(mp, kt, bias_ref[2])
        m = jnp.max(s, axis=-1, keepdims=True)
        p = jnp.exp(s - m)
        m_ref[mp] = m
        l_ref[mp] = _lane_block_sum(p)
        ps.append(p.astype(BF16))
        m_low = jnp.min(m, axis=0, keepdims=True)
        m_min = m_low if m_min is None else jnp.minimum(m_min, m_low)
    acc_ref[...] = _dot(jnp.concatenate(ps, axis=0), vt)

    rise = bound - m_min
    w = jnp.floor((rise + DA_SKIP_MARGIN) / (c * tk)) + 1.0
    w = jnp.minimum(w, float(nkv)).astype(jnp.int32)[0, 0]
    fast = jnp.where(rise < DA_FAST_RISE, 1, 0)[0, 0]
    lo = jnp.maximum(i - w, 0)
    n_other = jnp.minimum(i + w, nkv - 1) - lo

    def other_tile(t):
        j = lo + t
        j = j + jnp.where(j >= i, 1, 0)
        return tile(j) + (bias_ref[jnp.where(j < i, 0, 1)],)

    def fast_tiles(ts):
        pvs = []
        lsum = [None, None]
        for t in ts:
            kt, vt, cij, bias = other_tile(t)
            ps = []
            for mp in range(2):
                p = jnp.exp(scores(mp, kt, bias) - (m_ref[mp] - cij))
                part = _lane_block_sum(p)
                lsum[mp] = part if lsum[mp] is None else lsum[mp] + part
                ps.append(p.astype(BF16))
            pvs.append(_dot(jnp.concatenate(ps, axis=0), vt))
        for mp in range(2):
            l_ref[mp] += lsum[mp]
        acc_ref[...] += functools.reduce(lambda x, y: x + y, pvs)

    def fast_pair(t, carry):
        fast_tiles((2 * t, 2 * t + 1))
        return carry

    def fast_single(t, carry):
        fast_tiles((t,))
        return carry

    def safe_body(t, carry):
        kt, vt, cij, bias = other_tile(t)
        ps, alphas = [], []
        for mp in range(2):
            s = scores(mp, kt, bias)
            m_old = m_ref[mp]
            m_new = jnp.maximum(m_old, jnp.max(s, axis=-1, keepdims=True) + cij)
            alpha = jnp.exp(m_old - m_new)
            p = jnp.exp(s - (m_new - cij))
            l_ref[mp] = alpha * l_ref[mp] + _lane_block_sum(p)
            m_ref[mp] = m_new
            ps.append(p.astype(BF16))
            alphas.append(alpha)
        pv = _dot(jnp.concatenate(ps, axis=0), vt)
        for mp in range(2):
            rows = slice(mp * tq, (mp + 1) * tq)
            acc_ref[rows, :] = alphas[mp] * acc_ref[rows, :] + pv[rows]
        return carry

    n_fast = jnp.where(fast > 0, n_other, 0)
    lax.fori_loop(0, n_fast // 2, fast_pair, 0)
    lax.fori_loop(n_fast - n_fast % 2, n_fast, fast_single, 0)
    lax.fori_loop(0, n_other - n_fast, safe_body, 0)

    lp = lam_ref[...]
    lam = (jnp.exp(jnp.sum(lp[0:1] * lp[1:2], axis=-1, keepdims=True))
           - jnp.exp(jnp.sum(lp[2:3] * lp[3:4], axis=-1, keepdims=True)) + lam_init)
    l0 = jnp.sum(l_ref[0], axis=-1, keepdims=True)
    l1 = jnp.sum(l_ref[1], axis=-1, keepdims=True)
    o = acc_ref[0:tq, :] / l0 - lam * (acc_ref[tq:2 * tq, :] / l1)
    y = o * lax.rsqrt(jnp.mean(o * o, axis=-1, keepdims=True) + EPS) * ng_ref[...]
    o_ref[...] = (y * (1.0 - lam_init)).astype(o_ref.dtype)


def _diff_attention(proj, lam_params, norm_g, layer, d_hgrn, d_diff, *, tile=512):
    t = proj.shape[0]
    heads = d_diff // DA_V_DIM
    tq = tk = min(tile, t)
    assert t % tq == 0 and tk % 128 == 0
    nq = t // tq
    lam_init = 0.8 - 0.6 * math.exp(-0.3 * layer)
    base = 5 * d_hgrn // DA_V_DIM
    slopes = np.exp2(-8.0 * np.arange(1, heads + 1, dtype=np.float32) / heads).astype(np.float32)
    slopes = jnp.asarray(np.broadcast_to(slopes[:, None, None], (heads, 1, 128)))

    return pl.pallas_call(
        functools.partial(_diff_attn_kernel, lam_init=lam_init, tk=tk, chunk=min(1024, t)),
        grid=(heads, nq),
        in_specs=[pl.BlockSpec((tq, DA_V_DIM), lambda h, i: (i, base + h)),
                  pl.BlockSpec((t, DA_V_DIM), lambda h, i: (0, base + heads + h)),
                  pl.BlockSpec((t, DA_V_DIM), lambda h, i: (0, base + 2 * heads + h)),
                  pl.BlockSpec((1, 1, 128), lambda h, i: (h, 0, 0)),
                  pl.BlockSpec((4, DA_HEAD_DIM), lambda h, i: (0, 0)),
                  pl.BlockSpec((1, DA_V_DIM), lambda h, i: (0, 0))],
        out_specs=pl.BlockSpec((tq, DA_V_DIM), lambda h, i: (i, h)),
        out_shape=jax.ShapeDtypeStruct((t, d_diff), BF16),
        scratch_shapes=[pltpu.VMEM((tq, DA_V_DIM), BF16),
                        pltpu.VMEM((3, tq, tk), F32),
                        pltpu.VMEM((8, 128), F32),
                        pltpu.VMEM((2, tq, 1), F32),
                        pltpu.VMEM((2, tq, 128), F32),
                        pltpu.VMEM((2 * tq, DA_V_DIM), F32)],
        compiler_params=_params("parallel", "arbitrary"),
    )(proj, proj, proj, slopes, lam_params, norm_g.reshape(1, DA_V_DIM))


def _conv_glu_kernel(ug_ref, uv_ref, pg_ref, pv_ref, ng_ref, nv_ref, cwg_ref, cwv_ref, cbg_ref, cbv_ref, o_ref):
    i = pl.program_id(0)
    tr = ug_ref.shape[0]
    has_prev = jnp.where(i > 0, 1.0, 0.0)
    has_next = jnp.where(i < pl.num_programs(0) - 1, 1.0, 0.0)
    row = lax.broadcasted_iota(jnp.int32, ug_ref.shape, 0)

    def conv(u_ref, p_ref, n_ref, cw_ref, cb_ref):
        u = u_ref[...].astype(F32)
        prev = p_ref[7:8, :].astype(F32) * has_prev
        nxt = n_ref[0:1, :].astype(F32) * has_next
        up = jnp.where(row == 0, prev, pltpu.roll(u, 1, axis=0))
        un = jnp.where(row == tr - 1, nxt, pltpu.roll(u, tr - 1, axis=0))
        cw = cw_ref[...]
        return up * cw[0:1] + u * cw[1:2] + un * cw[2:3] + cb_ref[...]

    gate = conv(ug_ref, pg_ref, ng_ref, cwg_ref, cbg_ref)
    val = conv(uv_ref, pv_ref, nv_ref, cwv_ref, cbv_ref)
    o_ref[...] = (gate * _sigmoid(gate) * val).astype(o_ref.dtype)


def _conv_glu(u, conv_w, conv_b, *, tr=256, tc=1024):
    t, n2 = u.shape
    dff = n2 // 2
    tr, tc = min(tr, t), min(tc, dff)
    assert t % tr == 0 and dff % tc == 0 and tr % 8 == 0
    ncb = dff // tc
    r8 = tr // 8
    last8 = t // 8 - 1

    def main(off):
        return pl.BlockSpec((tr, tc), lambda i, j: (i, off + j))

    def prev(off):
        return pl.BlockSpec((8, tc), lambda i, j: (jnp.maximum(i * r8 - 1, 0), off + j))

    def nxt(off):
        return pl.BlockSpec((8, tc), lambda i, j: (jnp.minimum((i + 1) * r8, last8), off + j))

    def prm(rows, off):
        return pl.BlockSpec((rows, tc), lambda i, j: (0, off + j))

    cb = conv_b.reshape(1, n2)
    return pl.pallas_call(
        _conv_glu_kernel,
        grid=(t // tr, ncb),
        in_specs=[main(0), main(ncb), prev(0), prev(ncb), nxt(0), nxt(ncb),
                  prm(CONV_WIDTH, 0), prm(CONV_WIDTH, ncb), prm(1, 0), prm(1, ncb)],
        out_specs=pl.BlockSpec((tr, tc), lambda i, j: (i, j)),
        out_shape=jax.ShapeDtypeStruct((t, dff), BF16),
        compiler_params=_params("parallel", "parallel"),
    )(u, u, u, u, u, u, conv_w, conv_w, cb, cb)


def kernel(x, c, w_ada, b_ada, ada_table, norm1_g, w_in, hg_lb_logits, hg_norm_g, da_lambda, da_norm_g,
           w_out, norm2_g, w_up, conv_w, conv_b, w_down, final_g):
    bsz, t, d = x.shape
    assert bsz == 1
    depth = w_in.shape[0]
    d_hgrn = hg_lb_logits.shape[-1]
    d_diff = w_out.shape[1] - d_hgrn
    assert w_in.shape[2] == 5 * d_hgrn + 3 * d_diff

    mod = _ada_mod(c, w_ada, b_ada, ada_table)
    w_in, w_out, w_up, w_down = (w.astype(BF16) for w in (w_in, w_out, w_up, w_down))
    xs = x.reshape(t, d)
    for l in range(depth):
        m = [mod[l, i * d:(i + 1) * d] for i in range(N_MOD)]
        h = _rmsnorm(xs, norm1_g[l], m[0], m[1])
        proj = _matmul([h], w_in, l, tm=1024, tn=1024, tk=d)
        o_fwd = _hgrn_pass(proj, hg_lb_logits, l, d_hgrn, reverse=False)
        o_hg = _hgrn_pass(proj, hg_lb_logits, l, d_hgrn, reverse=True, o_fwd=o_fwd, norm_g=hg_norm_g[l])
        o_da = _diff_attention(proj, da_lambda[l], da_norm_g[l], l, d_hgrn, d_diff)
        if d_hgrn == d_diff:
            mix = [o_hg, o_da]
        else:
            mix = [jnp.concatenate([o_hg, o_da], axis=1)]
        xs = _matmul(mix, w_out, l, tm=1024, tn=512, tk=2048, x=xs, gate=m[2])
        h = _rmsnorm(xs, norm2_g[l], m[3], m[4])
        u = _matmul([h], w_up, l, tm=1024, tn=1024, tk=d)
        act = _conv_glu(u, conv_w[l], conv_b[l])
        xs = _matmul([act], w_down, l, tm=1024, tn=1024, tk=2048, x=xs, gate=m[5])
    out = _rmsnorm(xs, final_g, out_dtype=F32)
    return out.reshape(bsz, t, d)
```

```python
import functools
import math

import jax
import jax.numpy as jnp
import numpy as np
from jax import lax
from jax.experimental import pallas as pl
from jax.experimental.pallas import tpu as pltpu

F32 = jnp.float32
BF16 = jnp.bfloat16

HG_HEAD_DIM = 128
DA_HEAD_DIM = 128
DA_V_DIM = 2 * DA_HEAD_DIM
N_MOD = 6
CONV_WIDTH = 3
EPS = 1e-6
HG_LEAF = 16
HG_LEAF_CLAMP = 80.0
DA_SKIP_MARGIN = 110.0
DA_FAST_RISE = 60.0
VMEM_LIMIT_BYTES = 56 * 1024 * 1024


def _params(*semantics):
    return pltpu.CompilerParams(dimension_semantics=semantics, vmem_limit_bytes=VMEM_LIMIT_BYTES)


def _dot(a, b):
    return lax.dot_general(a, b, (((1,), (0,)), ((), ())), preferred_element_type=F32)


def _dot_nt(a, b):
    return lax.dot_general(a, b, (((1,), (1,)), ((), ())), preferred_element_type=F32)


def _dot_tn(a, b):
    return lax.dot_general(a, b, (((0,), (0,)), ((), ())), preferred_element_type=F32)


def _sigmoid(x):
    return 1.0 / (1.0 + jnp.exp(-x))


def _pick(n, *cands):
    for c in cands:
        if n % c == 0:
            return c
    return n


def _mod_kernel(c_ref, w_ref, b_ref, t_ref, o_ref):
    c = c_ref[...]
    s = c * _sigmoid(c)
    y = jnp.dot(s, w_ref[...], preferred_element_type=F32, precision=lax.Precision.HIGHEST)
    o_ref[...] = y[0:1, :] + b_ref[...] + t_ref[...]


def _ada_mod(c, w_ada, b_ada, ada_table):
    _, d = c.shape
    depth = ada_table.shape[0]
    n = w_ada.shape[1]
    tn = _pick(n, 512, 256, 128)
    c8 = jnp.broadcast_to(c, (8, d))
    return pl.pallas_call(
        _mod_kernel,
        grid=(n // tn,),
        in_specs=[pl.BlockSpec((8, d), lambda j: (0, 0)),
                  pl.BlockSpec((d, tn), lambda j: (0, j)),
                  pl.BlockSpec((1, tn), lambda j: (0, j)),
                  pl.BlockSpec((depth, tn), lambda j: (0, j))],
        out_specs=pl.BlockSpec((depth, tn), lambda j: (0, j)),
        out_shape=jax.ShapeDtypeStruct((depth, n), F32),
        compiler_params=_params("parallel"),
    )(c8, w_ada, b_ada.reshape(1, n), ada_table.reshape(depth, n))


def _norm_kernel(x_ref, g_ref, *rest, modulated):
    if modulated:
        shift_ref, scale_ref, o_ref = rest
    else:
        (o_ref,) = rest
    x = x_ref[...]
    y = x * lax.rsqrt(jnp.mean(x * x, axis=-1, keepdims=True) + EPS) * g_ref[...]
    if modulated:
        y = y * (1.0 + scale_ref[...]) + shift_ref[...]
    o_ref[...] = y.astype(o_ref.dtype)


def _rmsnorm(x, g, shift=None, scale=None, out_dtype=BF16):
    t, d = x.shape
    tr = _pick(t, 256, 128, 64, 32, 16, 8)
    modulated = shift is not None
    row = pl.BlockSpec((1, d), lambda i: (0, 0))
    args = [x, g.reshape(1, d)] + ([shift.reshape(1, d), scale.reshape(1, d)] if modulated else [])
    return pl.pallas_call(
        functools.partial(_norm_kernel, modulated=modulated),
        grid=(t // tr,),
        in_specs=[pl.BlockSpec((tr, d), lambda i: (i, 0))] + [row] * (len(args) - 1),
        out_specs=pl.BlockSpec((tr, d), lambda i: (i, 0)),
        out_shape=jax.ShapeDtypeStruct((t, d), out_dtype),
        compiler_params=_params("parallel"),
    )(*args)


def _matmul_kernel(*refs, n_a, residual):
    a_refs = refs[:n_a]
    w_ref = refs[n_a]
    o_ref = refs[-1]
    ka = a_refs[0].shape[1]
    acc = _dot(a_refs[0][...], w_ref[0:ka, :])
    for p in range(1, n_a):
        acc = acc + _dot(a_refs[p][...], w_ref[p * ka:(p + 1) * ka, :])
    if residual:
        x_ref, gate_ref = refs[n_a + 1], refs[n_a + 2]
        o_ref[...] = x_ref[...] + gate_ref[...] * acc
    else:
        o_ref[...] = acc.astype(o_ref.dtype)


def _matmul(a_list, w, layer, *, tm, tn, out_dtype=BF16, x=None, gate=None):
    m, ka = a_list[0].shape
    n_a = len(a_list)
    _, kt, n = w.shape
    assert kt == n_a * ka and all(a.shape == (m, ka) for a in a_list)
    tm, tn = min(tm, m), min(tn, n)
    assert m % tm == 0 and n % tn == 0
    residual = x is not None
    in_specs = [pl.BlockSpec((tm, ka), lambda i, j: (i, 0)) for _ in range(n_a)]
    in_specs.append(pl.BlockSpec((None, kt, tn), lambda i, j: (layer, 0, j)))
    args = list(a_list) + [w]
    if residual:
        in_specs += [pl.BlockSpec((tm, tn), lambda i, j: (i, j)),
                     pl.BlockSpec((1, tn), lambda i, j: (0, j))]
        args += [x, gate.reshape(1, n)]
        out_dtype = F32
    return pl.pallas_call(
        functools.partial(_matmul_kernel, n_a=n_a, residual=residual),
        grid=(m // tm, n // tn),
        in_specs=in_specs,
        out_specs=pl.BlockSpec((tm, tn), lambda i, j: (i, j)),
        out_shape=jax.ShapeDtypeStruct((m, n), out_dtype),
        compiler_params=_params("parallel", "parallel"),
    )(*args)


def _hgrn_level_ids(r, reverse):
    t = np.arange(r)[:, None]
    s = np.arange(r)[None, :]
    ids = np.full((r, r), -1, np.int32)
    tri = (s >= t) if reverse else (s <= t)
    ids[(t // HG_LEAF == s // HG_LEAF) & tri] = 0
    lv, i = HG_LEAF, 1
    while 2 * lv <= r:
        same = (t // (2 * lv)) == (s // (2 * lv))
        t_hi, s_hi = (t & lv) != 0, (s & lv) != 0
        pair = (~t_hi & s_hi) if reverse else (t_hi & ~s_hi)
        ids[same & pair] = i
        lv, i = 2 * lv, i + 1
    return ids


def _hgrn_kernel(q_ref, z_ref, v_ref, lbl_ref, ids_ref, *rest, reverse, layer, heads, final):
    if final:
        ofwd_ref, gate_ref, ng_ref, o_ref, st_ref = rest
    else:
        o_ref, st_ref = rest
    r = q_ref.shape[0]
    hd = HG_HEAD_DIM

    @pl.when(pl.program_id(1) == 0)
    def _():
        st_ref[...] = jnp.zeros_like(st_ref)

    logits = lbl_ref[0]
    e = jnp.exp(logits - jnp.max(logits, axis=0, keepdims=True))
    lb_all = (jnp.sum(e[1:layer + 1], axis=0, keepdims=True) / jnp.sum(e, axis=0, keepdims=True)
              if layer > 0 else jnp.zeros_like(e[0:1]))

    ids = ids_ref[...]
    n_levels = (r // HG_LEAF).bit_length()
    masks = [ids == level for level in range(n_levels)]
    row = lax.broadcasted_iota(jnp.int32, (r, r), 0)
    col = lax.broadcasted_iota(jnp.int32, (r, r), 1)
    tri = jnp.where((col >= row) if reverse else (col <= row), 1.0, 0.0).astype(BF16)
    total_row = 0 if reverse else r - 1

    for h in range(heads):
        sl = slice(h * hd, (h + 1) * hd)
        lb = lb_all[:, sl]
        q = q_ref[:, sl].astype(F32) * (hd ** -0.5)
        z = z_ref[:, sl].astype(F32)
        v = v_ref[:, sl]
        sig = _sigmoid(z)
        g = jnp.log(lb + (1.0 - lb) * sig)
        kc = (1.0 - lb) * (1.0 - sig)

        g1 = g.astype(BF16)
        rem = g - g1.astype(F32)
        g2 = rem.astype(BF16)
        g3 = (rem - g2.astype(F32)).astype(BF16)
        b = _dot(tri, g1) + _dot(tri, g2) + _dot(tri, g3)

        a = jnp.zeros((r, r), F32)
        lv, level = HG_LEAF // 2, 0
        while 2 * lv <= r:
            blk = 2 * lv
            nb = r // blk
            ref_row = lv if reverse else lv - 1
            bb = b.reshape(nb, blk, hd)
            d = bb - bb[:, ref_row:ref_row + 1, :]
            if level == 0:
                eq = jnp.exp(jnp.minimum(d, HG_LEAF_CLAMP))
                ek = jnp.exp(jnp.minimum(-d, HG_LEAF_CLAMP))
            else:
                eq = ek = jnp.exp(-jnp.abs(d))
            qe = (q.reshape(nb, blk, hd) * eq).reshape(r, hd).astype(BF16)
            ke = (kc.reshape(nb, blk, hd) * ek).reshape(r, hd).astype(BF16)
            a = jnp.where(masks[level], _dot_nt(qe, ke), a)
            lv, level = 2 * lv, level + 1
        assert level == n_levels

        st = st_ref[h]
        o = _dot(a.astype(BF16), v) + _dot_nt((q * jnp.exp(b)).astype(BF16), st.astype(BF16))
        bl = b[total_row:total_row + 1, :]
        kd = (kc * jnp.exp(bl - b)).astype(BF16)
        st_ref[h] = jnp.exp(bl) * st + _dot_tn(v, kd)

        if final:
            o = o + ofwd_ref[:, sl]
            y = o * lax.rsqrt(jnp.mean(o * o, axis=-1, keepdims=True) + EPS) * ng_ref[...]
            gt = gate_ref[:, sl].astype(F32)
            o_ref[:, sl] = (y * (gt * _sigmoid(gt))).astype(o_ref.dtype)
        else:
            o_ref[:, sl] = o


def _hgrn_pass(proj, lb_logits, layer, d_hgrn, *, reverse, o_fwd=None, norm_g=None, chunk=256):
    t = proj.shape[0]
    heads_total = d_hgrn // HG_HEAD_DIM
    hb = _pick(heads_total, 4, 2, 1)
    w = hb * HG_HEAD_DIM
    r = min(chunk, t)
    assert t % r == 0 and r % (2 * HG_LEAF) == 0
    nchunks = t // r
    cpb = d_hgrn // w
    final = o_fwd is not None
    depth = lb_logits.shape[1]

    def rows(n):
        return (nchunks - 1 - n) if reverse else n

    def col_spec(group):
        return pl.BlockSpec((r, w), lambda h, n: (rows(n), group * cpb + h))

    in_specs = [col_spec(0), col_spec(2 if reverse else 1), col_spec(3),
                pl.BlockSpec((1, depth, w), lambda h, n: (1 if reverse else 0, 0, h)),
                pl.BlockSpec((r, r), lambda h, n: (0, 0))]
    args = [proj, proj, proj, lb_logits, jnp.asarray(_hgrn_level_ids(r, reverse))]
    if final:
        in_specs += [pl.BlockSpec((r, w), lambda h, n: (rows(n), h)), col_spec(4),
                     pl.BlockSpec((1, HG_HEAD_DIM), lambda h, n: (0, 0))]
        args += [o_fwd, proj, norm_g.reshape(1, HG_HEAD_DIM)]
    return pl.pallas_call(
        functools.partial(_hgrn_kernel, reverse=reverse, layer=layer, heads=hb, final=final),
        grid=(heads_total // hb, nchunks),
        in_specs=in_specs,
        out_specs=pl.BlockSpec((r, w), lambda h, n: (rows(n), h)),
        out_shape=jax.ShapeDtypeStruct((t, d_hgrn), BF16 if final else F32),
        scratch_shapes=[pltpu.VMEM((hb, HG_HEAD_DIM, HG_HEAD_DIM), F32)],
        compiler_params=_params("parallel", "arbitrary"),
    )(*args)


def _scaled_q(q_ref):
    return (q_ref[...].astype(F32) * (DA_HEAD_DIM ** -0.5)).astype(BF16)


def _max_sq_norm(x):
    sq = x.astype(F32)
    sq = sq * sq
    n = jnp.maximum(jnp.sum(sq[:, :DA_HEAD_DIM], axis=-1, keepdims=True),
                    jnp.sum(sq[:, DA_HEAD_DIM:], axis=-1, keepdims=True))
    return jnp.max(n, axis=0, keepdims=True)


def _lane_block_sum(p):
    out = p[:, 0:128]
    for kb in range(1, p.shape[1] // 128):
        out = out + p[:, kb * 128:(kb + 1) * 128]
    return out


def _diff_attn_kernel(q_ref, k_ref, v_ref, slope_ref, lam_ref, ng_ref, o_ref,
                      qs_ref, bias_ref, kmax_ref, m_ref, l_ref, acc_ref, *, lam_init, tk, chunk):
    i = pl.program_id(1)
    tq = q_ref.shape[0]
    nkv = k_ref.shape[0] // tk
    hd = DA_HEAD_DIM
    c = slope_ref[0][0:1, 0:1]

    @pl.when(i == 0)
    def _():
        a = lax.broadcasted_iota(jnp.int32, (tq, tk), 0)
        b = lax.broadcasted_iota(jnp.int32, (tq, tk), 1)
        d = (a - b).astype(F32) * c
        bias_ref[0] = -d
        bias_ref[1] = d
        bias_ref[2] = -jnp.abs(d)

        def body(r, acc):
            rows = k_ref[pl.ds(pl.multiple_of(r * chunk, chunk), chunk), :]
            return jnp.maximum(acc, _max_sq_norm(rows))
        kmax = lax.fori_loop(0, k_ref.shape[0] // chunk, body, jnp.zeros((1, 1), F32))
        kmax_ref[...] = jnp.broadcast_to(kmax, kmax_ref.shape)

    qs = _scaled_q(q_ref)
    qs_ref[...] = qs
    bound = jnp.sqrt(_max_sq_norm(qs) * kmax_ref[0:1, 0:1])

    def tile(j):
        off = pl.multiple_of(j * tk, tk)
        cij = -c * jnp.abs(i * tq - j * tk).astype(F32)
        return k_ref[pl.ds(off, tk), :], v_ref[pl.ds(off, tk), :], cij

    def scores(mp, kt, bias):
        sl = slice(mp * hd, (mp + 1) * hd)
        return _dot_nt(qs_ref[:, sl], kt[:, sl]) + bias

    kt, vt, _ = tile(i)
    ps, m_min = [], None
    for mp in range(2):
        s = scores(mp, kt, bias_ref[2])
        m = jnp.max(s, axis=-1, keepdims=True)
        p = jnp.exp(s - m)
        m_ref[mp] = m
        l_ref[mp] = _lane_block_sum(p)
        ps.append(p.astype(BF16))
        m_low = jnp.min(m, axis=0, keepdims=True)
        m_min = m_low if m_min is None else jnp.minimum(m_min, m_low)
    acc_ref[...] = _dot(jnp.concatenate(ps, axis=0), vt)

    rise = bound - m_min
    w = jnp.floor((rise + DA_SKIP_MARGIN) / (c * tk)) + 1.0
    w = jnp.minimum(w, float(nkv)).astype(jnp.int32)[0, 0]
    fast = jnp.where(rise < DA_FAST_RISE, 1, 0)[0, 0]
    lo = jnp.maximum(i - w, 0)
    n_other = jnp.minimum(i + w, nkv - 1) - lo

    def other_tile(t):
        j = lo + t
        j = j + jnp.where(j >= i, 1, 0)
        return tile(j) + (bias_ref[jnp.where(j < i, 0, 1)],)

    def fast_tiles(ts):
        pvs = []
        lsum = [None, None]
        for t in ts:
            kt, vt, cij, bias = other_tile(t)
            ps = []
            for mp in range(2):
                p = jnp.exp(scores(mp, kt, bias) - (m_ref[mp] - cij))
                part = _lane_block_sum(p)
                lsum[mp] = part if lsum[mp] is None else lsum[mp] + part
                ps.append(p.astype(BF16))
            pvs.append(_dot(jnp.concatenate(ps, axis=0), vt))
        for mp in range(2):
            l_ref[mp] += lsum[mp]
        acc_ref[...] += functools.reduce(lambda x, y: x + y, pvs)

    def fast_pair(t, carry):
        fast_tiles((2 * t, 2 * t + 1))
        return carry

    def fast_single(t, carry):
        fast_tiles((t,))
        return carry

    def safe_body(t, carry):
        kt, vt, cij, bias = other_tile(t)
        ps, alphas = [], []
        for mp in range(2):
            s = scores(mp, kt, bias)
            m_old = m_ref[mp]
            m_new = jnp.maximum(m_old, jnp.max(s, axis=-1, keepdims=True) + cij)
            alpha = jnp.exp(m_old - m_new)
            p = jnp.exp(s - (m_new - cij))
            l_ref[mp] = alpha * l_ref[mp] + _lane_block_sum(p)
            m_ref[mp] = m_new
            ps.append(p.astype(BF16))
            alphas.append(alpha)
        pv = _dot(jnp.concatenate(ps, axis=0), vt)
        for mp in range(2):
            rows = slice(mp * tq, (mp + 1) * tq)
            acc_ref[rows, :] = alphas[mp] * acc_ref[rows, :] + pv[rows]
        return carry

    n_fast = jnp.where(fast > 0, n_other, 0)
    lax.fori_loop(0, n_fast // 2, fast_pair, 0)
    lax.fori_loop(n_fast - n_fast % 2, n_fast, fast_single, 0)
    lax.fori_loop(0, n_other - n_fast, safe_body, 0)

    lp = lam_ref[...]
    lam = (jnp.exp(jnp.sum(lp[0:1] * lp[1:2], axis=-1, keepdims=True))
           - jnp.exp(jnp.sum(lp[2:3] * lp[3:4], axis=-1, keepdims=True)) + lam_init)
    l0 = jnp.sum(l_ref[0], axis=-1, keepdims=True)
    l1 = jnp.sum(l_ref[1], axis=-1, keepdims=True)
    o = acc_ref[0:tq, :] / l0 - lam * (acc_ref[tq:2 * tq, :] / l1)
    y = o * lax.rsqrt(jnp.mean(o * o, axis=-1, keepdims=True) + EPS) * ng_ref[...]
    o_ref[...] = (y * (1.0 - lam_init)).astype(o_ref.dtype)


def _diff_attention(proj, lam_params, norm_g, layer, d_hgrn, d_diff, *, tile=512):
    t = proj.shape[0]
    heads = d_diff // DA_V_DIM
    tq = tk = min(tile, t)
    assert t % tq == 0 and tk % 128 == 0
    nq = t // tq
    lam_init = 0.8 - 0.6 * math.exp(-0.3 * layer)
    base = 5 * d_hgrn // DA_V_DIM
    slopes = np.exp2(-8.0 * np.arange(1, heads + 1, dtype=np.float32) / heads).astype(np.float32)
    slopes = jnp.asarray(np.broadcast_to(slopes[:, None, None], (heads, 1, 128)))

    return pl.pallas_call(
        functools.partial(_diff_attn_kernel, lam_init=lam_init, tk=tk, chunk=min(1024, t)),
        grid=(heads, nq),
        in_specs=[pl.BlockSpec((tq, DA_V_DIM), lambda h, i: (i, base + h)),
                  pl.BlockSpec((t, DA_V_DIM), lambda h, i: (0, base + heads + h)),
                  pl.BlockSpec((t, DA_V_DIM), lambda h, i: (0, base + 2 * heads + h)),
                  pl.BlockSpec((1, 1, 128), lambda h, i: (h, 0, 0)),
                  pl.BlockSpec((4, DA_HEAD_DIM), lambda h, i: (0, 0)),
                  pl.BlockSpec((1, DA_V_DIM), lambda h, i: (0, 0))],
        out_specs=pl.BlockSpec((tq, DA_V_DIM), lambda h, i: (i, h)),
        out_shape=jax.ShapeDtypeStruct((t, d_diff), BF16),
        scratch_shapes=[pltpu.VMEM((tq, DA_V_DIM), BF16),
                        pltpu.VMEM((3, tq, tk), F32),
                        pltpu.VMEM((8, 128), F32),
                        pltpu.VMEM((2, tq, 1), F32),
                        pltpu.VMEM((2, tq, 128), F32),
                        pltpu.VMEM((2 * tq, DA_V_DIM), F32)],
        compiler_params=_params("parallel", "arbitrary"),
    )(proj, proj, proj, slopes, lam_params, norm_g.reshape(1, DA_V_DIM))


def _conv_glu_kernel(ug_ref, uv_ref, pg_ref, pv_ref, ng_ref, nv_ref, cwg_ref, cwv_ref, cbg_ref, cbv_ref, o_ref):
    i = pl.program_id(0)
    tr = ug_ref.shape[0]
    has_prev = jnp.where(i > 0, 1.0, 0.0)
    has_next = jnp.where(i < pl.num_programs(0) - 1, 1.0, 0.0)
    row = lax.broadcasted_iota(jnp.int32, ug_ref.shape, 0)

    def conv(u_ref, p_ref, n_ref, cw_ref, cb_ref):
        u = u_ref[...].astype(F32)
        prev = p_ref[7:8, :].astype(F32) * has_prev
        nxt = n_ref[0:1, :].astype(F32) * has_next
        up = jnp.where(row == 0, prev, pltpu.roll(u, 1, axis=0))
        un = jnp.where(row == tr - 1, nxt, pltpu.roll(u, tr - 1, axis=0))
        cw = cw_ref[...]
        return up * cw[0:1] + u * cw[1:2] + un * cw[2:3] + cb_ref[...]

    gate = conv(ug_ref, pg_ref, ng_ref, cwg_ref, cbg_ref)
    val = conv(uv_ref, pv_ref, nv_ref, cwv_ref, cbv_ref)
    o_ref[...] = (gate * _sigmoid(gate) * val).astype(o_ref.dtype)


def _conv_glu(u, conv_w, conv_b, *, tr=256, tc=1024):
    t, n2 = u.shape
    dff = n2 // 2
    tr, tc = min(tr, t), min(tc, dff)
    assert t % tr == 0 and dff % tc == 0 and tr % 8 == 0
    ncb = dff // tc
    r8 = tr // 8
    last8 = t // 8 - 1

    def main(off):
        return pl.BlockSpec((tr, tc), lambda i, j: (i, off + j))

    def prev(off):
        return pl.BlockSpec((8, tc), lambda i, j: (jnp.maximum(i * r8 - 1, 0), off + j))

    def nxt(off):
        return pl.BlockSpec((8, tc), lambda i, j: (jnp.minimum((i + 1) * r8, last8), off + j))

    def prm(rows, off):
        return pl.BlockSpec((rows, tc), lambda i, j: (0, off + j))

    cb = conv_b.reshape(1, n2)
    return pl.pallas_call(
        _conv_glu_kernel,
        grid=(t // tr, ncb),
        in_specs=[main(0), main(ncb), prev(0), prev(ncb), nxt(0), nxt(ncb),
                  prm(CONV_WIDTH, 0), prm(CONV_WIDTH, ncb), prm(1, 0), prm(1, ncb)],
        out_specs=pl.BlockSpec((tr, tc), lambda i, j: (i, j)),
        out_shape=jax.ShapeDtypeStruct((t, dff), BF16),
        compiler_params=_params("parallel", "parallel"),
    )(u, u, u, u, u, u, conv_w, conv_w, cb, cb)


def kernel(x, c, w_ada, b_ada, ada_table, norm1_g, w_in, hg_lb_logits, hg_norm_g, da_lambda, da_norm_g,
           w_out, norm2_g, w_up, conv_w, conv_b, w_down, final_g):
    bsz, t, d = x.shape
    assert bsz == 1
    depth = w_in.shape[0]
    d_hgrn = hg_lb_logits.shape[-1]
    d_diff = w_out.shape[1] - d_hgrn
    assert w_in.shape[2] == 5 * d_hgrn + 3 * d_diff

    mod = _ada_mod(c, w_ada, b_ada, ada_table)
    w_in, w_out, w_up, w_down = (w.astype(BF16) for w in (w_in, w_out, w_up, w_down))
    xs = x.reshape(t, d)
    for l in range(depth):
        m = [mod[l, i * d:(i + 1) * d] for i in range(N_MOD)]
        h = _rmsnorm(xs, norm1_g[l], m[0], m[1])
        proj = _matmul([h], w_in, l, tm=1024, tn=1024)
        o_fwd = _hgrn_pass(proj, hg_lb_logits, l, d_hgrn, reverse=False)
        o_hg = _hgrn_pass(proj, hg_lb_logits, l, d_hgrn, reverse=True, o_fwd=o_fwd, norm_g=hg_norm_g[l])
        o_da = _diff_attention(proj, da_lambda[l], da_norm_g[l], l, d_hgrn, d_diff)
        if d_hgrn == d_diff:
            mix = [o_hg, o_da]
        else:
            mix = [jnp.concatenate([o_hg, o_da], axis=1)]
        xs = _matmul(mix, w_out, l, tm=1024, tn=512, x=xs, gate=m[2])
        h = _rmsnorm(xs, norm2_g[l], m[3], m[4])
        u = _matmul([h], w_up, l, tm=1024, tn=1024)
        act = _conv_glu(u, conv_w[l], conv_b[l])
        xs = _matmul([act], w_down, l, tm=512, tn=512, x=xs, gate=m[5])
    out = _rmsnorm(xs, final_g, out_dtype=F32)
    return out.reshape(bsz, t, d)
```

```python
import functools
import math

import jax
import jax.numpy as jnp
import numpy as np
from jax import lax
from jax.experimental import pallas as pl
from jax.experimental.pallas import tpu as pltpu

F32 = jnp.float32
BF16 = jnp.bfloat16

HG_HEAD_DIM = 128
DA_HEAD_DIM = 128
DA_V_DIM = 2 * DA_HEAD_DIM
N_MOD = 6
CONV_WIDTH = 3
HALO = 16
EPS = 1e-6
HG_LEAF = 16
HG_LEAF_CLAMP = 80.0
DA_SKIP_MARGIN = 110.0
DA_FAST_RISE = 60.0
DA_FAST_GROUP = 2
VMEM_LIMIT_BYTES = 56 * 1024 * 1024


def _params(*semantics):
    return pltpu.CompilerParams(dimension_semantics=semantics, vmem_limit_bytes=VMEM_LIMIT_BYTES)


def _dot(a, b):
    return lax.dot_general(a, b, (((1,), (0,)), ((), ())), preferred_element_type=F32)


def _dot_nt(a, b):
    return lax.dot_general(a, b, (((1,), (1,)), ((), ())), preferred_element_type=F32)


def _dot_tn(a, b):
    return lax.dot_general(a, b, (((0,), (0,)), ((), ())), preferred_element_type=F32)


def _sigmoid(x):
    return 1.0 / (1.0 + jnp.exp(-x))


def _pick(n, *cands):
    for c in cands:
        if n % c == 0:
            return c
    return n


def _mod_kernel(c_ref, w_ref, b_ref, t_ref, o_ref):
    c = c_ref[...]
    s = c * _sigmoid(c)
    y = jnp.dot(s, w_ref[...], preferred_element_type=F32, precision=lax.Precision.HIGHEST)
    o_ref[...] = y[0:1, :] + b_ref[...] + t_ref[...]


def _ada_mod(c, w_ada, b_ada, ada_table):
    _, d = c.shape
    depth = ada_table.shape[0]
    n = w_ada.shape[1]
    tn = _pick(n, 512, 256, 128)
    c8 = jnp.broadcast_to(c, (8, d))
    return pl.pallas_call(
        _mod_kernel,
        grid=(n // tn,),
        in_specs=[pl.BlockSpec((8, d), lambda j: (0, 0)),
                  pl.BlockSpec((d, tn), lambda j: (0, j)),
                  pl.BlockSpec((1, tn), lambda j: (0, j)),
                  pl.BlockSpec((depth, tn), lambda j: (0, j))],
        out_specs=pl.BlockSpec((depth, tn), lambda j: (0, j)),
        out_shape=jax.ShapeDtypeStruct((depth, n), F32),
        compiler_params=_params("parallel"),
    )(c8, w_ada, b_ada.reshape(1, n), ada_table.reshape(depth, n))


def _norm_kernel(x_ref, g_ref, *rest, modulated):
    if modulated:
        shift_ref, scale_ref, o_ref = rest
    else:
        (o_ref,) = rest
    x = x_ref[...]
    y = x * lax.rsqrt(jnp.mean(x * x, axis=-1, keepdims=True) + EPS) * g_ref[...]
    if modulated:
        y = y * (1.0 + scale_ref[...]) + shift_ref[...]
    o_ref[...] = y.astype(o_ref.dtype)


def _rmsnorm(x, g, shift=None, scale=None, out_dtype=BF16):
    t, d = x.shape
    tr = _pick(t, 256, 128, 64, 32, 16, 8)
    modulated = shift is not None
    row = pl.BlockSpec((1, d), lambda i: (0, 0))
    args = [x, g.reshape(1, d)] + ([shift.reshape(1, d), scale.reshape(1, d)] if modulated else [])
    return pl.pallas_call(
        functools.partial(_norm_kernel, modulated=modulated),
        grid=(t // tr,),
        in_specs=[pl.BlockSpec((tr, d), lambda i: (i, 0))] + [row] * (len(args) - 1),
        out_specs=pl.BlockSpec((tr, d), lambda i: (i, 0)),
        out_shape=jax.ShapeDtypeStruct((t, d), out_dtype),
        compiler_params=_params("parallel"),
    )(*args)


def _matmul_kernel(*refs, n_a, residual):
    a_refs = refs[:n_a]
    w_ref = refs[n_a]
    o_ref = refs[-1]
    ka = a_refs[0].shape[1]
    acc = _dot(a_refs[0][...], w_ref[0:ka, :])
    for p in range(1, n_a):
        acc = acc + _dot(a_refs[p][...], w_ref[p * ka:(p + 1) * ka, :])
    if residual:
        x_ref, gate_ref = refs[n_a + 1], refs[n_a + 2]
        o_ref[...] = x_ref[...] + gate_ref[...] * acc
    else:
        o_ref[...] = acc.astype(o_ref.dtype)


def _matmul(a_list, w, layer, *, tm, tn, out_dtype=BF16, x=None, gate=None):
    m, ka = a_list[0].shape
    n_a = len(a_list)
    _, kt, n = w.shape
    assert kt == n_a * ka and all(a.shape == (m, ka) for a in a_list)
    tm, tn = min(tm, m), min(tn, n)
    assert m % tm == 0 and n % tn == 0
    residual = x is not None
    in_specs = [pl.BlockSpec((tm, ka), lambda i, j: (i, 0)) for _ in range(n_a)]
    in_specs.append(pl.BlockSpec((None, kt, tn), lambda i, j: (layer, 0, j)))
    args = list(a_list) + [w]
    if residual:
        in_specs += [pl.BlockSpec((tm, tn), lambda i, j: (i, j)),
                     pl.BlockSpec((1, tn), lambda i, j: (0, j))]
        args += [x, gate.reshape(1, n)]
        out_dtype = F32
    return pl.pallas_call(
        functools.partial(_matmul_kernel, n_a=n_a, residual=residual),
        grid=(m // tm, n // tn),
        in_specs=in_specs,
        out_specs=pl.BlockSpec((tm, tn), lambda i, j: (i, j)),
        out_shape=jax.ShapeDtypeStruct((m, n), out_dtype),
        compiler_params=_params("parallel", "parallel"),
    )(*args)


def _hgrn_level_ids(r, reverse):
    t = np.arange(r)[:, None]
    s = np.arange(r)[None, :]
    ids = np.full((r, r), -1, np.int32)
    tri = (s >= t) if reverse else (s <= t)
    ids[(t // HG_LEAF == s // HG_LEAF) & tri] = 0
    lv, i = HG_LEAF, 1
    while 2 * lv <= r:
        same = (t // (2 * lv)) == (s // (2 * lv))
        t_hi, s_hi = (t & lv) != 0, (s & lv) != 0
        pair = (~t_hi & s_hi) if reverse else (t_hi & ~s_hi)
        ids[same & pair] = i
        lv, i = 2 * lv, i + 1
    return ids


def _hgrn_kernel(q_ref, z_ref, v_ref, lbl_ref, ids_ref, *rest, reverse, layer, heads, final):
    if final:
        ofwd_ref, gate_ref, ng_ref, o_ref, st_ref = rest
    else:
        o_ref, st_ref = rest
    r = q_ref.shape[0]
    hd = HG_HEAD_DIM

    @pl.when(pl.program_id(1) == 0)
    def _():
        st_ref[...] = jnp.zeros_like(st_ref)

    logits = lbl_ref[0]
    e = jnp.exp(logits - jnp.max(logits, axis=0, keepdims=True))
    lb_all = (jnp.sum(e[1:layer + 1], axis=0, keepdims=True) / jnp.sum(e, axis=0, keepdims=True)
              if layer > 0 else jnp.zeros_like(e[0:1]))

    ids = ids_ref[...]
    n_levels = (r // HG_LEAF).bit_length()
    masks = [ids == level for level in range(n_levels)]
    row = lax.broadcasted_iota(jnp.int32, (r, r), 0)
    col = lax.broadcasted_iota(jnp.int32, (r, r), 1)
    tri = jnp.where((col >= row) if reverse else (col <= row), 1.0, 0.0).astype(BF16)
    total_row = 0 if reverse else r - 1

    for h in range(heads):
        sl = slice(h * hd, (h + 1) * hd)
        lb = lb_all[:, sl]
        q = q_ref[:, sl].astype(F32) * (hd ** -0.5)
        z = z_ref[:, sl].astype(F32)
        v = v_ref[:, sl]
        sig = _sigmoid(z)
        g = jnp.log(lb + (1.0 - lb) * sig)
        kc = (1.0 - lb) * (1.0 - sig)

        g1 = g.astype(BF16)
        rem = g - g1.astype(F32)
        g2 = rem.astype(BF16)
        g3 = (rem - g2.astype(F32)).astype(BF16)
        b = _dot(tri, g1) + _dot(tri, g2) + _dot(tri, g3)

        a = jnp.zeros((r, r), F32)
        lv, level = HG_LEAF // 2, 0
        while 2 * lv <= r:
            blk = 2 * lv
            nb = r // blk
            ref_row = lv if reverse else lv - 1
            bb = b.reshape(nb, blk, hd)
            d = bb - bb[:, ref_row:ref_row + 1, :]
            if level == 0:
                eq = jnp.exp(jnp.minimum(d, HG_LEAF_CLAMP))
                ek = jnp.exp(jnp.minimum(-d, HG_LEAF_CLAMP))
            else:
                eq = ek = jnp.exp(-jnp.abs(d))
            qe = (q.reshape(nb, blk, hd) * eq).reshape(r, hd).astype(BF16)
            ke = (kc.reshape(nb, blk, hd) * ek).reshape(r, hd).astype(BF16)
            a = jnp.where(masks[level], _dot_nt(qe, ke), a)
            lv, level = 2 * lv, level + 1
        assert level == n_levels

        st = st_ref[h]
        o = _dot(a.astype(BF16), v) + _dot_nt((q * jnp.exp(b)).astype(BF16), st.astype(BF16))
        bl = b[total_row:total_row + 1, :]
        kd = (kc * jnp.exp(bl - b)).astype(BF16)
        st_ref[h] = jnp.exp(bl) * st + _dot_tn(v, kd)

        if final:
            o = o + ofwd_ref[:, sl]
            y = o * lax.rsqrt(jnp.mean(o * o, axis=-1, keepdims=True) + EPS) * ng_ref[...]
            gt = gate_ref[:, sl].astype(F32)
            o_ref[:, sl] = (y * (gt * _sigmoid(gt))).astype(o_ref.dtype)
        else:
            o_ref[:, sl] = o


def _hgrn_pass(proj, lb_logits, layer, d_hgrn, *, reverse, o_fwd=None, norm_g=None, chunk=256):
    t = proj.shape[0]
    heads_total = d_hgrn // HG_HEAD_DIM
    hb = _pick(heads_total, 4, 2, 1)
    w = hb * HG_HEAD_DIM
    r = min(chunk, t)
    assert t % r == 0 and r % (2 * HG_LEAF) == 0
    nchunks = t // r
    cpb = d_hgrn // w
    final = o_fwd is not None
    depth = lb_logits.shape[1]

    def rows(n):
        return (nchunks - 1 - n) if reverse else n

    def col_spec(group):
        return pl.BlockSpec((r, w), lambda h, n: (rows(n), group * cpb + h))

    in_specs = [col_spec(0), col_spec(2 if reverse else 1), col_spec(3),
                pl.BlockSpec((1, depth, w), lambda h, n: (1 if reverse else 0, 0, h)),
                pl.BlockSpec((r, r), lambda h, n: (0, 0))]
    args = [proj, proj, proj, lb_logits, jnp.asarray(_hgrn_level_ids(r, reverse))]
    if final:
        in_specs += [pl.BlockSpec((r, w), lambda h, n: (rows(n), h)), col_spec(4),
                     pl.BlockSpec((1, HG_HEAD_DIM), lambda h, n: (0, 0))]
        args += [o_fwd, proj, norm_g.reshape(1, HG_HEAD_DIM)]
    return pl.pallas_call(
        functools.partial(_hgrn_kernel, reverse=reverse, layer=layer, heads=hb, final=final),
        grid=(heads_total // hb, nchunks),
        in_specs=in_specs,
        out_specs=pl.BlockSpec((r, w), lambda h, n: (rows(n), h)),
        out_shape=jax.ShapeDtypeStruct((t, d_hgrn), BF16 if final else F32),
        scratch_shapes=[pltpu.VMEM((hb, HG_HEAD_DIM, HG_HEAD_DIM), F32)],
        compiler_params=_params("parallel", "arbitrary"),
    )(*args)


def _scaled_q(q_ref):
    return (q_ref[...].astype(F32) * (DA_HEAD_DIM ** -0.5)).astype(BF16)


def _max_sq_norm(x):
    sq = x.astype(F32)
    sq = sq * sq
    n = jnp.maximum(jnp.sum(sq[:, :DA_HEAD_DIM], axis=-1, keepdims=True),
                    jnp.sum(sq[:, DA_HEAD_DIM:], axis=-1, keepdims=True))
    return jnp.max(n, axis=0, keepdims=True)


def _lane_block_sum(p):
    out = p[:, 0:128]
    for kb in range(1, p.shape[1] // 128):
        out = out + p[:, kb * 128:(kb + 1) * 128]
    return out


def _diff_attn_kernel(q_ref, k_ref, v_ref, slope_ref, lam_ref, ng_ref, o_ref,
                      qs_ref, bias_ref, kmax_ref, m_ref, l_ref, acc_ref, *, lam_init, tk, chunk):
    i = pl.program_id(1)
    tq = q_ref.shape[0]
    nkv = k_ref.shape[0] // tk
    hd = DA_HEAD_DIM
    c = slope_ref[0][0:1, 0:1]

    @pl.when(i == 0)
    def _():
        a = lax.broadcasted_iota(jnp.int32, (tq, tk), 0)
        b = lax.broadcasted_iota(jnp.int32, (tq, tk), 1)
        d = (a - b).astype(F32) * c
        bias_ref[0] = -d
        bias_ref[1] = d
        bias_ref[2] = -jnp.abs(d)

        def body(r, acc):
            rows = k_ref[pl.ds(pl.multiple_of(r * chunk, chunk), chunk), :]
            return jnp.maximum(acc, _max_sq_norm(rows))
        kmax = lax.fori_loop(0, k_ref.shape[0] // chunk, body, jnp.zeros((1, 1), F32))
        kmax_ref[...] = jnp.broadcast_to(kmax, kmax_ref.shape)

    qs = _scaled_q(q_ref)
    qs_ref[...] = qs
    bound = jnp.sqrt(_max_sq_norm(qs) * kmax_ref[0:1, 0:1])

    def tile(j):
        off = pl.multiple_of(j * tk, tk)
        cij = -c * jnp.abs(i * tq - j * tk).astype(F32)
        return k_ref[pl.ds(off, tk), :], v_ref[pl.ds(off, tk), :], cij

    def scores(mp, kt, bias):
        sl = slice(mp * hd, (mp + 1) * hd)
        return _dot_nt(qs_ref[:, sl], kt[:, sl]) + bias

    kt, vt, _ = tile(i)
    ps, m_min = [], None
    for mp in range(2):
        s = scores(mp, kt, bias_ref[2])
        m = jnp.max(s, axis=-1, keepdims=True)
        p = jnp.exp(s - m)
        m_ref[mp] = m
        l_ref[mp] = _lane_block_sum(p)
        ps.append(p.astype(BF16))
        m_low = jnp.min(m, axis=0, keepdims=True)
        m_min = m_low if m_min is None else jnp.minimum(m_min, m_low)
    acc_ref[...] = _dot(jnp.concatenate(ps, axis=0), vt)

    rise = bound - m_min
    w = jnp.floor((rise + DA_SKIP_MARGIN) / (c * tk)) + 1.0
    w = jnp.minimum(w, float(nkv)).astype(jnp.int32)[0, 0]
    fast = jnp.where(rise < DA_FAST_RISE, 1, 0)[0, 0]
    lo = jnp.maximum(i - w, 0)
    n_other = jnp.minimum(i + w, nkv - 1) - lo

    def other_tile(t):
        j = lo + t
        j = j + jnp.where(j >= i, 1, 0)
        return tile(j) + (bias_ref[jnp.where(j < i, 0, 1)],)

    def fast_tiles(ts):
        pvs = []
        lsum = [None, None]
        for t in ts:
            kt, vt, cij, bias = other_tile(t)
            ps = []
            for mp in range(2):
                p = jnp.exp(scores(mp, kt, bias) - (m_ref[mp] - cij))
                part = _lane_block_sum(p)
                lsum[mp] = part if lsum[mp] is None else lsum[mp] + part
                ps.append(p.astype(BF16))
            pvs.append(_dot(jnp.concatenate(ps, axis=0), vt))
        for mp in range(2):
            l_ref[mp] += lsum[mp]
        acc_ref[...] += functools.reduce(lambda x, y: x + y, pvs)

    def fast_group(t, carry):
        fast_tiles(tuple(DA_FAST_GROUP * t + g for g in range(DA_FAST_GROUP)))
        return carry

    def fast_single(t, carry):
        fast_tiles((t,))
        return carry

    def safe_body(t, carry):
        kt, vt, cij, bias = other_tile(t)
        ps, alphas = [], []
        for mp in range(2):
            s = scores(mp, kt, bias)
            m_old = m_ref[mp]
            m_new = jnp.maximum(m_old, jnp.max(s, axis=-1, keepdims=True) + cij)
            alpha = jnp.exp(m_old - m_new)
            p = jnp.exp(s - (m_new - cij))
            l_ref[mp] = alpha * l_ref[mp] + _lane_block_sum(p)
            m_ref[mp] = m_new
            ps.append(p.astype(BF16))
            alphas.append(alpha)
        pv = _dot(jnp.concatenate(ps, axis=0), vt)
        for mp in range(2):
            rows = slice(mp * tq, (mp + 1) * tq)
            acc_ref[rows, :] = alphas[mp] * acc_ref[rows, :] + pv[rows]
        return carry

    n_fast = jnp.where(fast > 0, n_other, 0)
    lax.fori_loop(0, n_fast // DA_FAST_GROUP, fast_group, 0)
    lax.fori_loop(n_fast - n_fast % DA_FAST_GROUP, n_fast, fast_single, 0)
    lax.fori_loop(0, n_other - n_fast, safe_body, 0)

    lp = lam_ref[...]
    lam = (jnp.exp(jnp.sum(lp[0:1] * lp[1:2], axis=-1, keepdims=True))
           - jnp.exp(jnp.sum(lp[2:3] * lp[3:4], axis=-1, keepdims=True)) + lam_init)
    l0 = jnp.sum(l_ref[0], axis=-1, keepdims=True)
    l1 = jnp.sum(l_ref[1], axis=-1, keepdims=True)
    o = acc_ref[0:tq, :] / l0 - lam * (acc_ref[tq:2 * tq, :] / l1)
    y = o * lax.rsqrt(jnp.mean(o * o, axis=-1, keepdims=True) + EPS) * ng_ref[...]
    o_ref[...] = (y * (1.0 - lam_init)).astype(o_ref.dtype)


def _diff_attention(proj, lam_params, norm_g, layer, d_hgrn, d_diff, *, tile=512):
    t = proj.shape[0]
    heads = d_diff // DA_V_DIM
    tq = tk = min(tile, t)
    assert t % tq == 0 and tk % 128 == 0
    nq = t // tq
    lam_init = 0.8 - 0.6 * math.exp(-0.3 * layer)
    base = 5 * d_hgrn // DA_V_DIM
    slopes = np.exp2(-8.0 * np.arange(1, heads + 1, dtype=np.float32) / heads).astype(np.float32)
    slopes = jnp.asarray(np.broadcast_to(slopes[:, None, None], (heads, 1, 128)))

    return pl.pallas_call(
        functools.partial(_diff_attn_kernel, lam_init=lam_init, tk=tk, chunk=min(1024, t)),
        grid=(heads, nq),
        in_specs=[pl.BlockSpec((tq, DA_V_DIM), lambda h, i: (i, base + h)),
                  pl.BlockSpec((t, DA_V_DIM), lambda h, i: (0, base + heads + h)),
                  pl.BlockSpec((t, DA_V_DIM), lambda h, i: (0, base + 2 * heads + h)),
                  pl.BlockSpec((1, 1, 128), lambda h, i: (h, 0, 0)),
                  pl.BlockSpec((4, DA_HEAD_DIM), lambda h, i: (0, 0)),
                  pl.BlockSpec((1, DA_V_DIM), lambda h, i: (0, 0))],
        out_specs=pl.BlockSpec((tq, DA_V_DIM), lambda h, i: (i, h)),
        out_shape=jax.ShapeDtypeStruct((t, d_diff), BF16),
        scratch_shapes=[pltpu.VMEM((tq, DA_V_DIM), BF16),
                        pltpu.VMEM((3, tq, tk), F32),
                        pltpu.VMEM((8, 128), F32),
                        pltpu.VMEM((2, tq, 1), F32),
                        pltpu.VMEM((2, tq, 128), F32),
                        pltpu.VMEM((2 * tq, DA_V_DIM), F32)],
        compiler_params=_params("parallel", "arbitrary"),
    )(proj, proj, proj, slopes, lam_params, norm_g.reshape(1, DA_V_DIM))


def _up_conv_glu_kernel(h_ref, hp_ref, hn_ref, wg_ref, wv_ref, cwg_ref, cwv_ref, cbg_ref, cbv_ref,
                        o_ref, hext_ref):
    i, j = pl.program_id(0), pl.program_id(1)
    tm = h_ref.shape[0]
    ext = hext_ref.shape[0]

    @pl.when(j == 0)
    def _():
        hext_ref[HALO:HALO + tm, :] = h_ref[...]

        @pl.when(i > 0)
        def _():
            hext_ref[0:HALO, :] = hp_ref[...]

        @pl.when(i == 0)
        def _():
            hext_ref[0:HALO, :] = jnp.zeros_like(hp_ref)

        @pl.when(i < pl.num_programs(0) - 1)
        def _():
            hext_ref[HALO + tm:ext, :] = hn_ref[...]

        @pl.when(i == pl.num_programs(0) - 1)
        def _():
            hext_ref[HALO + tm:ext, :] = jnp.zeros_like(hn_ref)

    hext = hext_ref[...]

    def conv(w_ref, cw_ref, cb_ref):
        u = _dot(hext, w_ref[...])
        up = pltpu.roll(u, 1, axis=0)[HALO:HALO + tm]
        un = pltpu.roll(u, ext - 1, axis=0)[HALO:HALO + tm]
        cw = cw_ref[...]
        return up * cw[0:1] + u[HALO:HALO + tm] * cw[1:2] + un * cw[2:3] + cb_ref[...]

    gate = conv(wg_ref, cwg_ref, cbg_ref)
    val = conv(wv_ref, cwv_ref, cbv_ref)
    o_ref[...] = (gate * _sigmoid(gate) * val).astype(o_ref.dtype)


def _up_conv_glu(h, w_up, layer, conv_w, conv_b, *, tm=1024, tn=512):
    t, d = h.shape
    n2 = w_up.shape[2]
    dff = n2 // 2
    tm, tn = min(tm, t), min(tn, dff)
    assert t % tm == 0 and dff % tn == 0 and tm % HALO == 0
    ncb = dff // tn
    rh = tm // HALO
    last = t // HALO - 1

    def wcol(off):
        return pl.BlockSpec((None, d, tn), lambda i, j: (layer, 0, off + j))

    def prm(rows, off):
        return pl.BlockSpec((rows, tn), lambda i, j: (0, off + j))

    cb = conv_b.reshape(1, n2)
    return pl.pallas_call(
        _up_conv_glu_kernel,
        grid=(t // tm, ncb),
        in_specs=[pl.BlockSpec((tm, d), lambda i, j: (i, 0)),
                  pl.BlockSpec((HALO, d), lambda i, j: (jnp.maximum(i * rh - 1, 0), 0)),
                  pl.BlockSpec((HALO, d), lambda i, j: (jnp.minimum((i + 1) * rh, last), 0)),
                  wcol(0), wcol(ncb),
                  prm(CONV_WIDTH, 0), prm(CONV_WIDTH, ncb), prm(1, 0), prm(1, ncb)],
        out_specs=pl.BlockSpec((tm, tn), lambda i, j: (i, j)),
        out_shape=jax.ShapeDtypeStruct((t, dff), BF16),
        scratch_shapes=[pltpu.VMEM((tm + 2 * HALO, d), BF16)],
        compiler_params=_params("parallel", "arbitrary"),
    )(h, h, h, w_up, w_up, conv_w, conv_w, cb, cb)


def kernel(x, c, w_ada, b_ada, ada_table, norm1_g, w_in, hg_lb_logits, hg_norm_g, da_lambda, da_norm_g,
           w_out, norm2_g, w_up, conv_w, conv_b, w_down, final_g):
    bsz, t, d = x.shape
    assert bsz == 1
    depth = w_in.shape[0]
    d_hgrn = hg_lb_logits.shape[-1]
    d_diff = w_out.shape[1] - d_hgrn
    assert w_in.shape[2] == 5 * d_hgrn + 3 * d_diff

    mod = _ada_mod(c, w_ada, b_ada, ada_table)
    w_in, w_out, w_up, w_down = (w.astype(BF16) for w in (w_in, w_out, w_up, w_down))
    xs = x.reshape(t, d)
    for l in range(depth):
        m = [mod[l, i * d:(i + 1) * d] for i in range(N_MOD)]
        h = _rmsnorm(xs, norm1_g[l], m[0], m[1])
        proj = _matmul([h], w_in, l, tm=1024, tn=1024)
        o_fwd = _hgrn_pass(proj, hg_lb_logits, l, d_hgrn, reverse=False)
        o_hg = _hgrn_pass(proj, hg_lb_logits, l, d_hgrn, reverse=True, o_fwd=o_fwd, norm_g=hg_norm_g[l])
        o_da = _diff_attention(proj, da_lambda[l], da_norm_g[l], l, d_hgrn, d_diff)
        if d_hgrn == d_diff:
            mix = [o_hg, o_da]
        else:
            mix = [jnp.concatenate([o_hg, o_da], axis=1)]
        xs = _matmul(mix, w_out, l, tm=1024, tn=512, x=xs, gate=m[2])
        h = _rmsnorm(xs, norm2_g[l], m[3], m[4])
        act = _up_conv_glu(h, w_up, l, conv_w[l], conv_b[l])
        xs = _matmul([act], w_down, l, tm=512, tn=512, x=xs, gate=m[5])
    out = _rmsnorm(xs, final_g, out_dtype=F32)
    return out.reshape(bsz, t, d)
```

```python
import functools
import math

import jax
import jax.numpy as jnp
import numpy as np
from jax import lax
from jax.experimental import pallas as pl
from jax.experimental.pallas import tpu as pltpu

F32 = jnp.float32
BF16 = jnp.bfloat16

HG_HEAD_DIM = 128
DA_HEAD_DIM = 128
DA_V_DIM = 2 * DA_HEAD_DIM
N_MOD = 6
CONV_WIDTH = 3
HALO = 16
EPS = 1e-6
HG_LEAF = 16
HG_LEAF_CLAMP = 80.0
DA_SKIP_MARGIN = 104.5
DA_FAST_RISE = 60.0
DA_FAST_GROUP = 2
VMEM_LIMIT_BYTES = 56 * 1024 * 1024


def _params(*semantics):
    return pltpu.CompilerParams(dimension_semantics=semantics, vmem_limit_bytes=VMEM_LIMIT_BYTES)


def _dot(a, b):
    return lax.dot_general(a, b, (((1,), (0,)), ((), ())), preferred_element_type=F32)


def _dot_nt(a, b):
    return lax.dot_general(a, b, (((1,), (1,)), ((), ())), preferred_element_type=F32)


def _dot_tn(a, b):
    return lax.dot_general(a, b, (((0,), (0,)), ((), ())), preferred_element_type=F32)


def _sigmoid(x):
    return 1.0 / (1.0 + jnp.exp(-x))


def _pick(n, *cands):
    for c in cands:
        if n % c == 0:
            return c
    return n


def _mod_kernel(c_ref, w_ref, b_ref, t_ref, o_ref):
    c = c_ref[...]
    s = c * _sigmoid(c)
    y = jnp.dot(s, w_ref[...], preferred_element_type=F32, precision=lax.Precision.HIGHEST)
    o_ref[...] = y[0:1, :] + b_ref[...] + t_ref[...]


def _ada_mod(c, w_ada, b_ada, ada_table):
    _, d = c.shape
    depth = ada_table.shape[0]
    n = w_ada.shape[1]
    tn = _pick(n, 512, 256, 128)
    c8 = jnp.broadcast_to(c, (8, d))
    return pl.pallas_call(
        _mod_kernel,
        grid=(n // tn,),
        in_specs=[pl.BlockSpec((8, d), lambda j: (0, 0)),
                  pl.BlockSpec((d, tn), lambda j: (0, j)),
                  pl.BlockSpec((1, tn), lambda j: (0, j)),
                  pl.BlockSpec((depth, tn), lambda j: (0, j))],
        out_specs=pl.BlockSpec((depth, tn), lambda j: (0, j)),
        out_shape=jax.ShapeDtypeStruct((depth, n), F32),
        compiler_params=_params("parallel"),
    )(c8, w_ada, b_ada.reshape(1, n), ada_table.reshape(depth, n))


def _norm_kernel(x_ref, g_ref, *rest, modulated):
    if modulated:
        shift_ref, scale_ref, o_ref = rest
    else:
        (o_ref,) = rest
    x = x_ref[...]
    y = x * lax.rsqrt(jnp.mean(x * x, axis=-1, keepdims=True) + EPS) * g_ref[...]
    if modulated:
        y = y * (1.0 + scale_ref[...]) + shift_ref[...]
    o_ref[...] = y.astype(o_ref.dtype)


def _rmsnorm(x, g, shift=None, scale=None, out_dtype=BF16):
    t, d = x.shape
    tr = _pick(t, 256, 128, 64, 32, 16, 8)
    modulated = shift is not None
    row = pl.BlockSpec((1, d), lambda i: (0, 0))
    args = [x, g.reshape(1, d)] + ([shift.reshape(1, d), scale.reshape(1, d)] if modulated else [])
    return pl.pallas_call(
        functools.partial(_norm_kernel, modulated=modulated),
        grid=(t // tr,),
        in_specs=[pl.BlockSpec((tr, d), lambda i: (i, 0))] + [row] * (len(args) - 1),
        out_specs=pl.BlockSpec((tr, d), lambda i: (i, 0)),
        out_shape=jax.ShapeDtypeStruct((t, d), out_dtype),
        compiler_params=_params("parallel"),
    )(*args)


def _matmul_kernel(*refs, n_a, residual):
    a_refs = refs[:n_a]
    w_ref = refs[n_a]
    o_ref = refs[-1]
    ka = a_refs[0].shape[1]
    acc = _dot(a_refs[0][...], w_ref[0:ka, :])
    for p in range(1, n_a):
        acc = acc + _dot(a_refs[p][...], w_ref[p * ka:(p + 1) * ka, :])
    if residual:
        x_ref, gate_ref = refs[n_a + 1], refs[n_a + 2]
        o_ref[...] = x_ref[...] + gate_ref[...] * acc
    else:
        o_ref[...] = acc.astype(o_ref.dtype)


def _matmul(a_list, w, layer, *, tm, tn, out_dtype=BF16, x=None, gate=None):
    m, ka = a_list[0].shape
    n_a = len(a_list)
    _, kt, n = w.shape
    assert kt == n_a * ka and all(a.shape == (m, ka) for a in a_list)
    tm, tn = min(tm, m), min(tn, n)
    assert m % tm == 0 and n % tn == 0
    residual = x is not None
    in_specs = [pl.BlockSpec((tm, ka), lambda i, j: (i, 0)) for _ in range(n_a)]
    in_specs.append(pl.BlockSpec((None, kt, tn), lambda i, j: (layer, 0, j)))
    args = list(a_list) + [w]
    if residual:
        in_specs += [pl.BlockSpec((tm, tn), lambda i, j: (i, j)),
                     pl.BlockSpec((1, tn), lambda i, j: (0, j))]
        args += [x, gate.reshape(1, n)]
        out_dtype = F32
    return pl.pallas_call(
        functools.partial(_matmul_kernel, n_a=n_a, residual=residual),
        grid=(m // tm, n // tn),
        in_specs=in_specs,
        out_specs=pl.BlockSpec((tm, tn), lambda i, j: (i, j)),
        out_shape=jax.ShapeDtypeStruct((m, n), out_dtype),
        compiler_params=_params("parallel", "parallel"),
    )(*args)


def _hgrn_level_ids(r, reverse):
    t = np.arange(r)[:, None]
    s = np.arange(r)[None, :]
    ids = np.full((r, r), -1, np.int32)
    tri = (s >= t) if reverse else (s <= t)
    ids[(t // HG_LEAF == s // HG_LEAF) & tri] = 0
    lv, i = HG_LEAF, 1
    while 2 * lv <= r:
        same = (t // (2 * lv)) == (s // (2 * lv))
        t_hi, s_hi = (t & lv) != 0, (s & lv) != 0
        pair = (~t_hi & s_hi) if reverse else (t_hi & ~s_hi)
        ids[same & pair] = i
        lv, i = 2 * lv, i + 1
    return ids


def _hgrn_kernel(q_ref, z_ref, v_ref, lbl_ref, ids_ref, *rest, reverse, layer, heads, final):
    if final:
        ofwd_ref, gate_ref, ng_ref, o_ref, st_ref = rest
    else:
        o_ref, st_ref = rest
    r = q_ref.shape[0]
    hd = HG_HEAD_DIM

    @pl.when(pl.program_id(1) == 0)
    def _():
        st_ref[...] = jnp.zeros_like(st_ref)

    logits = lbl_ref[0]
    e = jnp.exp(logits - jnp.max(logits, axis=0, keepdims=True))
    lb_all = (jnp.sum(e[1:layer + 1], axis=0, keepdims=True) / jnp.sum(e, axis=0, keepdims=True)
              if layer > 0 else jnp.zeros_like(e[0:1]))

    ids = ids_ref[...]
    n_levels = (r // HG_LEAF).bit_length()
    masks = [ids == level for level in range(n_levels)]
    row = lax.broadcasted_iota(jnp.int32, (r, r), 0)
    col = lax.broadcasted_iota(jnp.int32, (r, r), 1)
    tri = jnp.where((col >= row) if reverse else (col <= row), 1.0, 0.0).astype(BF16)
    total_row = 0 if reverse else r - 1

    for h in range(heads):
        sl = slice(h * hd, (h + 1) * hd)
        lb = lb_all[:, sl]
        q = q_ref[:, sl].astype(F32) * (hd ** -0.5)
        z = z_ref[:, sl].astype(F32)
        v = v_ref[:, sl]
        sig = _sigmoid(z)
        g = jnp.log(lb + (1.0 - lb) * sig)
        kc = (1.0 - lb) * (1.0 - sig)

        g1 = g.astype(BF16)
        rem = g - g1.astype(F32)
        g2 = rem.astype(BF16)
        g3 = (rem - g2.astype(F32)).astype(BF16)
        b = _dot(tri, g1) + _dot(tri, g2) + _dot(tri, g3)

        a = jnp.zeros((r, r), F32)
        lv, level = HG_LEAF // 2, 0
        while 2 * lv <= r:
            blk = 2 * lv
            nb = r // blk
            ref_row = lv if reverse else lv - 1
            bb = b.reshape(nb, blk, hd)
            d = bb - bb[:, ref_row:ref_row + 1, :]
            if level == 0:
                eq = jnp.exp(jnp.minimum(d, HG_LEAF_CLAMP))
                ek = jnp.exp(jnp.minimum(-d, HG_LEAF_CLAMP))
            else:
                eq = ek = jnp.exp(-jnp.abs(d))
            qe = (q.reshape(nb, blk, hd) * eq).reshape(r, hd).astype(BF16)
            ke = (kc.reshape(nb, blk, hd) * ek).reshape(r, hd).astype(BF16)
            a = jnp.where(masks[level], _dot_nt(qe, ke), a)
            lv, level = 2 * lv, level + 1
        assert level == n_levels

        st = st_ref[h]
        o = _dot(a.astype(BF16), v) + _dot_nt((q * jnp.exp(b)).astype(BF16), st.astype(BF16))
        bl = b[total_row:total_row + 1, :]
        kd = (kc * jnp.exp(bl - b)).astype(BF16)
        st_ref[h] = jnp.exp(bl) * st + _dot_tn(v, kd)

        if final:
            o = o + ofwd_ref[:, sl]
            y = o * lax.rsqrt(jnp.mean(o * o, axis=-1, keepdims=True) + EPS) * ng_ref[...]
            gt = gate_ref[:, sl].astype(F32)
            o_ref[:, sl] = (y * (gt * _sigmoid(gt))).astype(o_ref.dtype)
        else:
            o_ref[:, sl] = o


def _hgrn_pass(proj, lb_logits, layer, d_hgrn, *, reverse, o_fwd=None, norm_g=None, chunk=256):
    t = proj.shape[0]
    heads_total = d_hgrn // HG_HEAD_DIM
    hb = _pick(heads_total, 8, 4, 2, 1)
    w = hb * HG_HEAD_DIM
    r = min(chunk, t)
    assert t % r == 0 and r % (2 * HG_LEAF) == 0
    nchunks = t // r
    cpb = d_hgrn // w
    final = o_fwd is not None
    depth = lb_logits.shape[1]

    def rows(n):
        return (nchunks - 1 - n) if reverse else n

    def col_spec(group):
        return pl.BlockSpec((r, w), lambda h, n: (rows(n), group * cpb + h))

    in_specs = [col_spec(0), col_spec(2 if reverse else 1), col_spec(3),
                pl.BlockSpec((1, depth, w), lambda h, n: (1 if reverse else 0, 0, h)),
                pl.BlockSpec((r, r), lambda h, n: (0, 0))]
    args = [proj, proj, proj, lb_logits, jnp.asarray(_hgrn_level_ids(r, reverse))]
    if final:
        in_specs += [pl.BlockSpec((r, w), lambda h, n: (rows(n), h)), col_spec(4),
                     pl.BlockSpec((1, HG_HEAD_DIM), lambda h, n: (0, 0))]
        args += [o_fwd, proj, norm_g.reshape(1, HG_HEAD_DIM)]
    return pl.pallas_call(
        functools.partial(_hgrn_kernel, reverse=reverse, layer=layer, heads=hb, final=final),
        grid=(heads_total // hb, nchunks),
        in_specs=in_specs,
        out_specs=pl.BlockSpec((r, w), lambda h, n: (rows(n), h)),
        out_shape=jax.ShapeDtypeStruct((t, d_hgrn), BF16 if final else F32),
        scratch_shapes=[pltpu.VMEM((hb, HG_HEAD_DIM, HG_HEAD_DIM), F32)],
        compiler_params=_params("parallel", "arbitrary"),
    )(*args)


def _scaled_q(q_ref):
    return (q_ref[...].astype(F32) * (DA_HEAD_DIM ** -0.5)).astype(BF16)


def _max_sq_norm(x):
    sq = x.astype(F32)
    sq = sq * sq
    n = jnp.maximum(jnp.sum(sq[:, :DA_HEAD_DIM], axis=-1, keepdims=True),
                    jnp.sum(sq[:, DA_HEAD_DIM:], axis=-1, keepdims=True))
    return jnp.max(n, axis=0, keepdims=True)


def _lane_block_sum(p):
    out = p[:, 0:128]
    for kb in range(1, p.shape[1] // 128):
        out = out + p[:, kb * 128:(kb + 1) * 128]
    return out


def _diff_attn_kernel(q_ref, k_ref, v_ref, slope_ref, lam_ref, ng_ref, o_ref,
                      qs_ref, bias_ref, kmax_ref, m_ref, l_ref, acc_ref, *, lam_init, tk, chunk):
    i = pl.program_id(1)
    tq = q_ref.shape[0]
    nkv = k_ref.shape[0] // tk
    hd = DA_HEAD_DIM
    c = slope_ref[0][0:1, 0:1]

    @pl.when(i == 0)
    def _():
        a = lax.broadcasted_iota(jnp.int32, (tq, tk), 0)
        b = lax.broadcasted_iota(jnp.int32, (tq, tk), 1)
        d = (a - b).astype(F32) * c
        bias_ref[0] = -d
        bias_ref[1] = d
        bias_ref[2] = -jnp.abs(d)

        def body(r, acc):
            rows = k_ref[pl.ds(pl.multiple_of(r * chunk, chunk), chunk), :]
            return jnp.maximum(acc, _max_sq_norm(rows))
        kmax = lax.fori_loop(0, k_ref.shape[0] // chunk, body, jnp.zeros((1, 1), F32))
        kmax_ref[...] = jnp.broadcast_to(kmax, kmax_ref.shape)

    qs = _scaled_q(q_ref)
    qs_ref[...] = qs
    bound = jnp.sqrt(_max_sq_norm(qs) * kmax_ref[0:1, 0:1])

    def tile(j):
        off = pl.multiple_of(j * tk, tk)
        cij = -c * jnp.abs(i * tq - j * tk).astype(F32)
        return k_ref[pl.ds(off, tk), :], v_ref[pl.ds(off, tk), :], cij

    def scores(mp, kt, bias):
        sl = slice(mp * hd, (mp + 1) * hd)
        return _dot_nt(qs_ref[:, sl], kt[:, sl]) + bias

    kt, vt, _ = tile(i)
    ps, m_min = [], None
    for mp in range(2):
        s = scores(mp, kt, bias_ref[2])
        m = jnp.max(s, axis=-1, keepdims=True)
        p = jnp.exp(s - m)
        m_ref[mp] = m
        l_ref[mp] = _lane_block_sum(p)
        ps.append(p.astype(BF16))
        m_low = jnp.min(m, axis=0, keepdims=True)
        m_min = m_low if m_min is None else jnp.minimum(m_min, m_low)
    acc_ref[...] = _dot(jnp.concatenate(ps, axis=0), vt)

    rise = bound - m_min
    w = jnp.floor((rise + DA_SKIP_MARGIN) / (c * tk)) + 1.0
    w = jnp.minimum(w, float(nkv)).astype(jnp.int32)[0, 0]
    fast = jnp.where(rise < DA_FAST_RISE, 1, 0)[0, 0]
    lo = jnp.maximum(i - w, 0)
    n_other = jnp.minimum(i + w, nkv - 1) - lo

    def other_tile(t):
        j = lo + t
        j = j + jnp.where(j >= i, 1, 0)
        return tile(j) + (bias_ref[jnp.where(j < i, 0, 1)],)

    def fast_tiles(ts):
        probs, vals = [], []
        lsum = [None, None]
        for t in ts:
            kt, vt, cij, bias = other_tile(t)
            ps = []
            for mp in range(2):
                p = jnp.exp(scores(mp, kt, bias) - (m_ref[mp] - cij))
                part = _lane_block_sum(p)
                lsum[mp] = part if lsum[mp] is None else lsum[mp] + part
                ps.append(p.astype(BF16))
            probs.append(jnp.concatenate(ps, axis=0))
            vals.append(vt)
        for mp in range(2):
            l_ref[mp] += lsum[mp]
        acc_ref[...] += _dot(jnp.concatenate(probs, axis=1), jnp.concatenate(vals, axis=0))

    def fast_group(t, carry):
        fast_tiles(tuple(DA_FAST_GROUP * t + g for g in range(DA_FAST_GROUP)))
        return carry

    def fast_single(t, carry):
        fast_tiles((t,))
        return carry

    def safe_body(t, carry):
        kt, vt, cij, bias = other_tile(t)
        ps, alphas = [], []
        for mp in range(2):
            s = scores(mp, kt, bias)
            m_old = m_ref[mp]
            m_new = jnp.maximum(m_old, jnp.max(s, axis=-1, keepdims=True) + cij)
            alpha = jnp.exp(m_old - m_new)
            p = jnp.exp(s - (m_new - cij))
            l_ref[mp] = alpha * l_ref[mp] + _lane_block_sum(p)
            m_ref[mp] = m_new
            ps.append(p.astype(BF16))
            alphas.append(alpha)
        pv = _dot(jnp.concatenate(ps, axis=0), vt)
        for mp in range(2):
            rows = slice(mp * tq, (mp + 1) * tq)
            acc_ref[rows, :] = alphas[mp] * acc_ref[rows, :] + pv[rows]
        return carry

    n_fast = jnp.where(fast > 0, n_other, 0)
    lax.fori_loop(0, n_fast // DA_FAST_GROUP, fast_group, 0)
    lax.fori_loop(n_fast - n_fast % DA_FAST_GROUP, n_fast, fast_single, 0)
    lax.fori_loop(0, n_other - n_fast, safe_body, 0)

    lp = lam_ref[...]
    lam = (jnp.exp(jnp.sum(lp[0:1] * lp[1:2], axis=-1, keepdims=True))
           - jnp.exp(jnp.sum(lp[2:3] * lp[3:4], axis=-1, keepdims=True)) + lam_init)
    l0 = jnp.sum(l_ref[0], axis=-1, keepdims=True)
    l1 = jnp.sum(l_ref[1], axis=-1, keepdims=True)
    o = acc_ref[0:tq, :] / l0 - lam * (acc_ref[tq:2 * tq, :] / l1)
    y = o * lax.rsqrt(jnp.mean(o * o, axis=-1, keepdims=True) + EPS) * ng_ref[...]
    o_ref[...] = (y * (1.0 - lam_init)).astype(o_ref.dtype)


def _diff_attention(proj, lam_params, norm_g, layer, d_hgrn, d_diff, *, tile=512):
    t = proj.shape[0]
    heads = d_diff // DA_V_DIM
    tq = tk = min(tile, t)
    assert t % tq == 0 and tk % 128 == 0
    nq = t // tq
    lam_init = 0.8 - 0.6 * math.exp(-0.3 * layer)
    base = 5 * d_hgrn // DA_V_DIM
    slopes = np.exp2(-8.0 * np.arange(1, heads + 1, dtype=np.float32) / heads).astype(np.float32)
    slopes = jnp.asarray(np.broadcast_to(slopes[:, None, None], (heads, 1, 128)))

    return pl.pallas_call(
        functools.partial(_diff_attn_kernel, lam_init=lam_init, tk=tk, chunk=min(1024, t)),
        grid=(heads, nq),
        in_specs=[pl.BlockSpec((tq, DA_V_DIM), lambda h, i: (i, base + h)),
                  pl.BlockSpec((t, DA_V_DIM), lambda h, i: (0, base + heads + h)),
                  pl.BlockSpec((t, DA_V_DIM), lambda h, i: (0, base + 2 * heads + h)),
                  pl.BlockSpec((1, 1, 128), lambda h, i: (h, 0, 0)),
                  pl.BlockSpec((4, DA_HEAD_DIM), lambda h, i: (0, 0)),
                  pl.BlockSpec((1, DA_V_DIM), lambda h, i: (0, 0))],
        out_specs=pl.BlockSpec((tq, DA_V_DIM), lambda h, i: (i, h)),
        out_shape=jax.ShapeDtypeStruct((t, d_diff), BF16),
        scratch_shapes=[pltpu.VMEM((tq, DA_V_DIM), BF16),
                        pltpu.VMEM((3, tq, tk), F32),
                        pltpu.VMEM((8, 128), F32),
                        pltpu.VMEM((2, tq, 1), F32),
                        pltpu.VMEM((2, tq, 128), F32),
                        pltpu.VMEM((2 * tq, DA_V_DIM), F32)],
        compiler_params=_params("parallel", "arbitrary"),
    )(proj, proj, proj, slopes, lam_params, norm_g.reshape(1, DA_V_DIM))


def _up_conv_glu_kernel(h_ref, hp_ref, hn_ref, wg_ref, wv_ref, cwg_ref, cwv_ref, cbg_ref, cbv_ref,
                        o_ref, hext_ref):
    i, j = pl.program_id(0), pl.program_id(1)
    tm = h_ref.shape[0]
    ext = hext_ref.shape[0]

    @pl.when(j == 0)
    def _():
        hext_ref[HALO:HALO + tm, :] = h_ref[...]

        @pl.when(i > 0)
        def _():
            hext_ref[0:HALO, :] = hp_ref[...]

        @pl.when(i == 0)
        def _():
            hext_ref[0:HALO, :] = jnp.zeros_like(hp_ref)

        @pl.when(i < pl.num_programs(0) - 1)
        def _():
            hext_ref[HALO + tm:ext, :] = hn_ref[...]

        @pl.when(i == pl.num_programs(0) - 1)
        def _():
            hext_ref[HALO + tm:ext, :] = jnp.zeros_like(hn_ref)

    hext = hext_ref[...]

    def conv(w_ref, cw_ref, cb_ref):
        u = _dot(hext, w_ref[...])
        up = pltpu.roll(u, 1, axis=0)[HALO:HALO + tm]
        un = pltpu.roll(u, ext - 1, axis=0)[HALO:HALO + tm]
        cw = cw_ref[...]
        return up * cw[0:1] + u[HALO:HALO + tm] * cw[1:2] + un * cw[2:3] + cb_ref[...]

    gate = conv(wg_ref, cwg_ref, cbg_ref)
    val = conv(wv_ref, cwv_ref, cbv_ref)
    o_ref[...] = (gate * _sigmoid(gate) * val).astype(o_ref.dtype)


def _up_conv_glu(h, w_up, layer, conv_w, conv_b, *, tm=1024, tn=512):
    t, d = h.shape
    n2 = w_up.shape[2]
    dff = n2 // 2
    tm, tn = min(tm, t), min(tn, dff)
    assert t % tm == 0 and dff % tn == 0 and tm % HALO == 0
    ncb = dff // tn
    rh = tm // HALO
    last = t // HALO - 1

    def wcol(off):
        return pl.BlockSpec((None, d, tn), lambda i, j: (layer, 0, off + j))

    def prm(rows, off):
        return pl.BlockSpec((rows, tn), lambda i, j: (0, off + j))

    cb = conv_b.reshape(1, n2)
    return pl.pallas_call(
        _up_conv_glu_kernel,
        grid=(t // tm, ncb),
        in_specs=[pl.BlockSpec((tm, d), lambda i, j: (i, 0)),
                  pl.BlockSpec((HALO, d), lambda i, j: (jnp.maximum(i * rh - 1, 0), 0)),
                  pl.BlockSpec((HALO, d), lambda i, j: (jnp.minimum((i + 1) * rh, last), 0)),
                  wcol(0), wcol(ncb),
                  prm(CONV_WIDTH, 0), prm(CONV_WIDTH, ncb), prm(1, 0), prm(1, ncb)],
        out_specs=pl.BlockSpec((tm, tn), lambda i, j: (i, j)),
        out_shape=jax.ShapeDtypeStruct((t, dff), BF16),
        scratch_shapes=[pltpu.VMEM((tm + 2 * HALO, d), BF16)],
        compiler_params=_params("parallel", "arbitrary"),
    )(h, h, h, w_up, w_up, conv_w, conv_w, cb, cb)


def kernel(x, c, w_ada, b_ada, ada_table, norm1_g, w_in, hg_lb_logits, hg_norm_g, da_lambda, da_norm_g,
           w_out, norm2_g, w_up, conv_w, conv_b, w_down, final_g):
    bsz, t, d = x.shape
    assert bsz == 1
    depth = w_in.shape[0]
    d_hgrn = hg_lb_logits.shape[-1]
    d_diff = w_out.shape[1] - d_hgrn
    assert w_in.shape[2] == 5 * d_hgrn + 3 * d_diff

    mod = _ada_mod(c, w_ada, b_ada, ada_table)
    w_in, w_out, w_up, w_down = (w.astype(BF16) for w in (w_in, w_out, w_up, w_down))
    xs = x.reshape(t, d)
    for l in range(depth):
        m = [mod[l, i * d:(i + 1) * d] for i in range(N_MOD)]
        h = _rmsnorm(xs, norm1_g[l], m[0], m[1])
        proj = _matmul([h], w_in, l, tm=1024, tn=1024)
        o_fwd = _hgrn_pass(proj, hg_lb_logits, l, d_hgrn, reverse=False)
        o_hg = _hgrn_pass(proj, hg_lb_logits, l, d_hgrn, reverse=True, o_fwd=o_fwd, norm_g=hg_norm_g[l])
        o_da = _diff_attention(proj, da_lambda[l], da_norm_g[l], l, d_hgrn, d_diff)
        if d_hgrn == d_diff:
            mix = [o_hg, o_da]
        else:
            mix = [jnp.concatenate([o_hg, o_da], axis=1)]
        xs = _matmul(mix, w_out, l, tm=1024, tn=512, x=xs, gate=m[2])
        h = _rmsnorm(xs, norm2_g[l], m[3], m[4])
        act = _up_conv_glu(h, w_up, l, conv_w[l], conv_b[l])
        xs = _matmul([act], w_down, l, tm=512, tn=512, x=xs, gate=m[5])
    out = _rmsnorm(xs, final_g, out_dtype=F32)
    return out.reshape(bsz, t, d)
```

```python
import functools
import math

import jax
import jax.numpy as jnp
import numpy as np
from jax import lax
from jax.experimental import pallas as pl
from jax.experimental.pallas import tpu as pltpu

F32 = jnp.float32
BF16 = jnp.bfloat16

HG_HEAD_DIM = 128
DA_HEAD_DIM = 128
DA_V_DIM = 2 * DA_HEAD_DIM
N_MOD = 6
CONV_WIDTH = 3
HALO = 16
EPS = 1e-6
HG_LEAF = 16
HG_LEAF_CLAMP = 80.0
DA_SKIP_MARGIN = 104.5
DA_FAST_RISE = 60.0
DA_FAST_GROUP = 2
VMEM_LIMIT_BYTES = 56 * 1024 * 1024


def _params(*semantics):
    return pltpu.CompilerParams(dimension_semantics=semantics, vmem_limit_bytes=VMEM_LIMIT_BYTES)


def _dot(a, b):
    return lax.dot_general(a, b, (((1,), (0,)), ((), ())), preferred_element_type=F32)


def _dot_nt(a, b):
    return lax.dot_general(a, b, (((1,), (1,)), ((), ())), preferred_element_type=F32)


def _dot_tn(a, b):
    return lax.dot_general(a, b, (((0,), (0,)), ((), ())), preferred_element_type=F32)


def _sigmoid(x):
    return 1.0 / (1.0 + jnp.exp(-x))


def _pick(n, *cands):
    for c in cands:
        if n % c == 0:
            return c
    return n


def _mod_kernel(c_ref, w_ref, b_ref, t_ref, o_ref):
    c = c_ref[...]
    s = c * _sigmoid(c)
    y = jnp.dot(s, w_ref[...], preferred_element_type=F32, precision=lax.Precision.HIGHEST)
    o_ref[...] = y[0:1, :] + b_ref[...] + t_ref[...]


def _ada_mod(c, w_ada, b_ada, ada_table):
    _, d = c.shape
    depth = ada_table.shape[0]
    n = w_ada.shape[1]
    tn = _pick(n, 512, 256, 128)
    c8 = jnp.broadcast_to(c, (8, d))
    return pl.pallas_call(
        _mod_kernel,
        grid=(n // tn,),
        in_specs=[pl.BlockSpec((8, d), lambda j: (0, 0)),
                  pl.BlockSpec((d, tn), lambda j: (0, j)),
                  pl.BlockSpec((1, tn), lambda j: (0, j)),
                  pl.BlockSpec((depth, tn), lambda j: (0, j))],
        out_specs=pl.BlockSpec((depth, tn), lambda j: (0, j)),
        out_shape=jax.ShapeDtypeStruct((depth, n), F32),
        compiler_params=_params("parallel"),
    )(c8, w_ada, b_ada.reshape(1, n), ada_table.reshape(depth, n))


def _norm_kernel(x_ref, g_ref, *rest, modulated):
    if modulated:
        shift_ref, scale_ref, o_ref = rest
    else:
        (o_ref,) = rest
    x = x_ref[...]
    y = x * lax.rsqrt(jnp.mean(x * x, axis=-1, keepdims=True) + EPS) * g_ref[...]
    if modulated:
        y = y * (1.0 + scale_ref[...]) + shift_ref[...]
    o_ref[...] = y.astype(o_ref.dtype)


def _rmsnorm(x, g, shift=None, scale=None, out_dtype=BF16):
    t, d = x.shape
    tr = _pick(t, 256, 128, 64, 32, 16, 8)
    modulated = shift is not None
    row = pl.BlockSpec((1, d), lambda i: (0, 0))
    args = [x, g.reshape(1, d)] + ([shift.reshape(1, d), scale.reshape(1, d)] if modulated else [])
    return pl.pallas_call(
        functools.partial(_norm_kernel, modulated=modulated),
        grid=(t // tr,),
        in_specs=[pl.BlockSpec((tr, d), lambda i: (i, 0))] + [row] * (len(args) - 1),
        out_specs=pl.BlockSpec((tr, d), lambda i: (i, 0)),
        out_shape=jax.ShapeDtypeStruct((t, d), out_dtype),
        compiler_params=_params("parallel"),
    )(*args)


def _cast_job(w_all, layer, grid):
    _, k, n = w_all.shape
    gi, gj = grid
    assert k % gi == 0 and n % gj == 0
    bk, bn = k // gi, n // gj
    assert bk % 16 == 0 and bn % 128 == 0
    return (pl.BlockSpec((None, bk, bn), lambda i, j: (layer, i, j)),
            pl.BlockSpec((bk, bn), lambda i, j: (i, j)),
            jax.ShapeDtypeStruct((k, n), BF16))


def _matmul_kernel(*refs, n_a, residual, n_cast):
    a_refs = refs[:n_a]
    w_ref = refs[n_a]
    n_in = n_a + 1 + (2 if residual else 0) + n_cast
    o_ref = refs[n_in]
    ka = a_refs[0].shape[1]
    acc = _dot(a_refs[0][...], w_ref[0:ka, :])
    for p in range(1, n_a):
        acc = acc + _dot(a_refs[p][...], w_ref[p * ka:(p + 1) * ka, :])
    if residual:
        x_ref, gate_ref = refs[n_a + 1], refs[n_a + 2]
        o_ref[...] = x_ref[...] + gate_ref[...] * acc
    else:
        o_ref[...] = acc.astype(o_ref.dtype)
    for c in range(n_cast):
        refs[n_in + 1 + c][...] = refs[n_in - n_cast + c][...].astype(BF16)


def _matmul(a_list, w, *, tm, tn, out_dtype=BF16, x=None, gate=None, cast=()):
    m, ka = a_list[0].shape
    n_a = len(a_list)
    kt, n = w.shape
    assert kt == n_a * ka and all(a.shape == (m, ka) for a in a_list)
    tm, tn = min(tm, m), min(tn, n)
    assert m % tm == 0 and n % tn == 0
    grid = (m // tm, n // tn)
    residual = x is not None
    in_specs = [pl.BlockSpec((tm, ka), lambda i, j: (i, 0)) for _ in range(n_a)]
    in_specs.append(pl.BlockSpec((kt, tn), lambda i, j: (0, j)))
    args = list(a_list) + [w]
    if residual:
        in_specs += [pl.BlockSpec((tm, tn), lambda i, j: (i, j)),
                     pl.BlockSpec((1, tn), lambda i, j: (0, j))]
        args += [x, gate.reshape(1, n)]
        out_dtype = F32
    out_specs = [pl.BlockSpec((tm, tn), lambda i, j: (i, j))]
    out_shape = [jax.ShapeDtypeStruct((m, n), out_dtype)]
    for w_all, layer in cast:
        c_in, c_out, c_shape = _cast_job(w_all, layer, grid)
        in_specs.append(c_in)
        args.append(w_all)
        out_specs.append(c_out)
        out_shape.append(c_shape)
    res = pl.pallas_call(
        functools.partial(_matmul_kernel, n_a=n_a, residual=residual, n_cast=len(cast)),
        grid=grid,
        in_specs=in_specs,
        out_specs=out_specs,
        out_shape=out_shape,
        compiler_params=_params("parallel", "parallel"),
    )(*args)
    return res[0], list(res[1:])


def _hgrn_level_ids(r, reverse):
    t = np.arange(r)[:, None]
    s = np.arange(r)[None, :]
    ids = np.full((r, r), -1, np.int32)
    tri = (s >= t) if reverse else (s <= t)
    ids[(t // HG_LEAF == s // HG_LEAF) & tri] = 0
    lv, i = HG_LEAF, 1
    while 2 * lv <= r:
        same = (t // (2 * lv)) == (s // (2 * lv))
        t_hi, s_hi = (t & lv) != 0, (s & lv) != 0
        pair = (~t_hi & s_hi) if reverse else (t_hi & ~s_hi)
        ids[same & pair] = i
        lv, i = 2 * lv, i + 1
    return ids


def _hgrn_kernel(q_ref, z_ref, v_ref, lbl_ref, ids_ref, *rest, reverse, layer, heads, final):
    if final:
        ofwd_ref, gate_ref, ng_ref, o_ref, st_ref = rest
    else:
        o_ref, st_ref = rest
    r = q_ref.shape[0]
    hd = HG_HEAD_DIM

    @pl.when(pl.program_id(1) == 0)
    def _():
        st_ref[...] = jnp.zeros_like(st_ref)

    logits = lbl_ref[0]
    e = jnp.exp(logits - jnp.max(logits, axis=0, keepdims=True))
    lb_all = (jnp.sum(e[1:layer + 1], axis=0, keepdims=True) / jnp.sum(e, axis=0, keepdims=True)
              if layer > 0 else jnp.zeros_like(e[0:1]))

    ids = ids_ref[...]
    n_levels = (r // HG_LEAF).bit_length()
    masks = [ids == level for level in range(n_levels)]
    row = lax.broadcasted_iota(jnp.int32, (r, r), 0)
    col = lax.broadcasted_iota(jnp.int32, (r, r), 1)
    tri = jnp.where((col >= row) if reverse else (col <= row), 1.0, 0.0).astype(BF16)
    total_row = 0 if reverse else r - 1

    for h in range(heads):
        sl = slice(h * hd, (h + 1) * hd)
        lb = lb_all[:, sl]
        q = q_ref[:, sl].astype(F32) * (hd ** -0.5)
        z = z_ref[:, sl].astype(F32)
        v = v_ref[:, sl]
        sig = _sigmoid(z)
        g = jnp.log(lb + (1.0 - lb) * sig)
        kc = (1.0 - lb) * (1.0 - sig)

        g1 = g.astype(BF16)
        rem = g - g1.astype(F32)
        g2 = rem.astype(BF16)
        g3 = (rem - g2.astype(F32)).astype(BF16)
        b = _dot(tri, g1) + _dot(tri, g2) + _dot(tri, g3)

        a = jnp.zeros((r, r), F32)
        lv, level = HG_LEAF // 2, 0
        while 2 * lv <= r:
            blk = 2 * lv
            nb = r // blk
            ref_row = lv if reverse else lv - 1
            bb = b.reshape(nb, blk, hd)
            d = bb - bb[:, ref_row:ref_row + 1, :]
            if level == 0:
                eq = jnp.exp(jnp.minimum(d, HG_LEAF_CLAMP))
                ek = jnp.exp(jnp.minimum(-d, HG_LEAF_CLAMP))
            else:
                eq = ek = jnp.exp(-jnp.abs(d))
            qe = (q.reshape(nb, blk, hd) * eq).reshape(r, hd).astype(BF16)
            ke = (kc.reshape(nb, blk, hd) * ek).reshape(r, hd).astype(BF16)
            a = jnp.where(masks[level], _dot_nt(qe, ke), a)
            lv, level = 2 * lv, level + 1
        assert level == n_levels

        st = st_ref[h]
        o = _dot(a.astype(BF16), v) + _dot_nt((q * jnp.exp(b)).astype(BF16), st.astype(BF16))
        bl = b[total_row:total_row + 1, :]
        kd = (kc * jnp.exp(bl - b)).astype(BF16)
        st_ref[h] = jnp.exp(bl) * st + _dot_tn(v, kd)

        if final:
            o = o + ofwd_ref[:, sl]
            y = o * lax.rsqrt(jnp.mean(o * o, axis=-1, keepdims=True) + EPS) * ng_ref[...]
            gt = gate_ref[:, sl].astype(F32)
            o_ref[:, sl] = (y * (gt * _sigmoid(gt))).astype(o_ref.dtype)
        else:
            o_ref[:, sl] = o


def _hgrn_pass(proj, lb_logits, layer, d_hgrn, *, reverse, o_fwd=None, norm_g=None, chunk=256):
    t = proj.shape[0]
    heads_total = d_hgrn // HG_HEAD_DIM
    hb = _pick(heads_total, 8, 4, 2, 1)
    w = hb * HG_HEAD_DIM
    r = min(chunk, t)
    assert t % r == 0 and r % (2 * HG_LEAF) == 0
    nchunks = t // r
    cpb = d_hgrn // w
    final = o_fwd is not None
    depth = lb_logits.shape[1]

    def rows(n):
        return (nchunks - 1 - n) if reverse else n

    def col_spec(group):
        return pl.BlockSpec((r, w), lambda h, n: (rows(n), group * cpb + h))

    in_specs = [col_spec(0), col_spec(2 if reverse else 1), col_spec(3),
                pl.BlockSpec((1, depth, w), lambda h, n: (1 if reverse else 0, 0, h)),
                pl.BlockSpec((r, r), lambda h, n: (0, 0))]
    args = [proj, proj, proj, lb_logits, jnp.asarray(_hgrn_level_ids(r, reverse))]
    if final:
        in_specs += [pl.BlockSpec((r, w), lambda h, n: (rows(n), h)), col_spec(4),
                     pl.BlockSpec((1, HG_HEAD_DIM), lambda h, n: (0, 0))]
        args += [o_fwd, proj, norm_g.reshape(1, HG_HEAD_DIM)]
    return pl.pallas_call(
        functools.partial(_hgrn_kernel, reverse=reverse, layer=layer, heads=hb, final=final),
        grid=(heads_total // hb, nchunks),
        in_specs=in_specs,
        out_specs=pl.BlockSpec((r, w), lambda h, n: (rows(n), h)),
        out_shape=jax.ShapeDtypeStruct((t, d_hgrn), BF16 if final else F32),
        scratch_shapes=[pltpu.VMEM((hb, HG_HEAD_DIM, HG_HEAD_DIM), F32)],
        compiler_params=_params("parallel", "arbitrary"),
    )(*args)


def _scaled_q(q_ref):
    return (q_ref[...].astype(F32) * (DA_HEAD_DIM ** -0.5)).astype(BF16)


def _max_sq_norm(x):
    sq = x.astype(F32)
    sq = sq * sq
    n = jnp.maximum(jnp.sum(sq[:, :DA_HEAD_DIM], axis=-1, keepdims=True),
                    jnp.sum(sq[:, DA_HEAD_DIM:], axis=-1, keepdims=True))
    return jnp.max(n, axis=0, keepdims=True)


def _lane_block_sum(p):
    out = p[:, 0:128]
    for kb in range(1, p.shape[1] // 128):
        out = out + p[:, kb * 128:(kb + 1) * 128]
    return out


def _diff_attn_kernel(q_ref, k_ref, v_ref, slope_ref, lam_ref, ng_ref, o_ref,
                      qs_ref, bias_ref, kmax_ref, m_ref, l_ref, acc_ref, *, lam_init, tk, chunk):
    i = pl.program_id(1)
    tq = q_ref.shape[0]
    nkv = k_ref.shape[0] // tk
    hd = DA_HEAD_DIM
    c = slope_ref[0][0:1, 0:1]

    @pl.when(i == 0)
    def _():
        a = lax.broadcasted_iota(jnp.int32, (tq, tk), 0)
        b = lax.broadcasted_iota(jnp.int32, (tq, tk), 1)
        d = (a - b).astype(F32) * c
        bias_ref[0] = -d
        bias_ref[1] = d
        bias_ref[2] = -jnp.abs(d)

        def body(r, acc):
            rows = k_ref[pl.ds(pl.multiple_of(r * chunk, chunk), chunk), :]
            return jnp.maximum(acc, _max_sq_norm(rows))
        kmax = lax.fori_loop(0, k_ref.shape[0] // chunk, body, jnp.zeros((1, 1), F32))
        kmax_ref[...] = jnp.broadcast_to(kmax, kmax_ref.shape)

    qs = _scaled_q(q_ref)
    qs_ref[...] = qs
    bound = jnp.sqrt(_max_sq_norm(qs) * kmax_ref[0:1, 0:1])

    def tile(j):
        off = pl.multiple_of(j * tk, tk)
        cij = -c * jnp.abs(i * tq - j * tk).astype(F32)
        return k_ref[pl.ds(off, tk), :], v_ref[pl.ds(off, tk), :], cij

    def scores(mp, kt, bias):
        sl = slice(mp * hd, (mp + 1) * hd)
        return _dot_nt(qs_ref[:, sl], kt[:, sl]) + bias

    kt, vt, _ = tile(i)
    ps, m_min = [], None
    for mp in range(2):
        s = scores(mp, kt, bias_ref[2])
        m = jnp.max(s, axis=-1, keepdims=True)
        p = jnp.exp(s - m)
        m_ref[mp] = m
        l_ref[mp] = _lane_block_sum(p)
        ps.append(p.astype(BF16))
        m_low = jnp.min(m, axis=0, keepdims=True)
        m_min = m_low if m_min is None else jnp.minimum(m_min, m_low)
    acc_ref[...] = _dot(jnp.concatenate(ps, axis=0), vt)

    rise = bound - m_min
    w = jnp.floor((rise + DA_SKIP_MARGIN) / (c * tk)) + 1.0
    w = jnp.minimum(w, float(nkv)).astype(jnp.int32)[0, 0]
    fast = jnp.where(rise < DA_FAST_RISE, 1, 0)[0, 0]
    lo = jnp.maximum(i - w, 0)
    n_other = jnp.minimum(i + w, nkv - 1) - lo

    def other_tile(t):
        j = lo + t
        j = j + jnp.where(j >= i, 1, 0)
        return tile(j) + (bias_ref[jnp.where(j < i, 0, 1)],)

    def fast_tiles(ts):
        probs, vals = [], []
        lsum = [None, None]
        for t in ts:
            kt, vt, cij, bias = other_tile(t)
            ps = []
            for mp in range(2):
                p = jnp.exp(scores(mp, kt, bias) - (m_ref[mp] - cij))
                part = _lane_block_sum(p)
                lsum[mp] = part if lsum[mp] is None else lsum[mp] + part
                ps.append(p.astype(BF16))
            probs.append(jnp.concatenate(ps, axis=0))
            vals.append(vt)
        for mp in range(2):
            l_ref[mp] += lsum[mp]
        acc_ref[...] += _dot(jnp.concatenate(probs, axis=1), jnp.concatenate(vals, axis=0))

    def fast_group(t, carry):
        fast_tiles(tuple(DA_FAST_GROUP * t + g for g in range(DA_FAST_GROUP)))
        return carry

    def fast_single(t, carry):
        fast_tiles((t,))
        return carry

    def safe_body(t, carry):
        kt, vt, cij, bias = other_tile(t)
        ps, alphas = [], []
        for mp in range(2):
            s = scores(mp, kt, bias)
            m_old = m_ref[mp]
            m_new = jnp.maximum(m_old, jnp.max(s, axis=-1, keepdims=True) + cij)
            alpha = jnp.exp(m_old - m_new)
            p = jnp.exp(s - (m_new - cij))
            l_ref[mp] = alpha * l_ref[mp] + _lane_block_sum(p)
            m_ref[mp] = m_new
            ps.append(p.astype(BF16))
            alphas.append(alpha)
        pv = _dot(jnp.concatenate(ps, axis=0), vt)
        for mp in range(2):
            rows = slice(mp * tq, (mp + 1) * tq)
            acc_ref[rows, :] = alphas[mp] * acc_ref[rows, :] + pv[rows]
        return carry

    n_fast = jnp.where(fast > 0, n_other, 0)
    lax.fori_loop(0, n_fast // DA_FAST_GROUP, fast_group, 0)
    lax.fori_loop(n_fast - n_fast % DA_FAST_GROUP, n_fast, fast_single, 0)
    lax.fori_loop(0, n_other - n_fast, safe_body, 0)

    lp = lam_ref[...]
    lam = (jnp.exp(jnp.sum(lp[0:1] * lp[1:2], axis=-1, keepdims=True))
           - jnp.exp(jnp.sum(lp[2:3] * lp[3:4], axis=-1, keepdims=True)) + lam_init)
    l0 = jnp.sum(l_ref[0], axis=-1, keepdims=True)
    l1 = jnp.sum(l_ref[1], axis=-1, keepdims=True)
    o = acc_ref[0:tq, :] / l0 - lam * (acc_ref[tq:2 * tq, :] / l1)
    y = o * lax.rsqrt(jnp.mean(o * o, axis=-1, keepdims=True) + EPS) * ng_ref[...]
    o_ref[...] = (y * (1.0 - lam_init)).astype(o_ref.dtype)


def _diff_attention(proj, lam_params, norm_g, layer, d_hgrn, d_diff, *, tile=512):
    t = proj.shape[0]
    heads = d_diff // DA_V_DIM
    tq = tk = min(tile, t)
    assert t % tq == 0 and tk % 128 == 0
    nq = t // tq
    lam_init = 0.8 - 0.6 * math.exp(-0.3 * layer)
    base = 5 * d_hgrn // DA_V_DIM
    slopes = np.exp2(-8.0 * np.arange(1, heads + 1, dtype=np.float32) / heads).astype(np.float32)
    slopes = jnp.asarray(np.broadcast_to(slopes[:, None, None], (heads, 1, 128)))

    return pl.pallas_call(
        functools.partial(_diff_attn_kernel, lam_init=lam_init, tk=tk, chunk=min(1024, t)),
        grid=(heads, nq),
        in_specs=[pl.BlockSpec((tq, DA_V_DIM), lambda h, i: (i, base + h)),
                  pl.BlockSpec((t, DA_V_DIM), lambda h, i: (0, base + heads + h)),
                  pl.BlockSpec((t, DA_V_DIM), lambda h, i: (0, base + 2 * heads + h)),
                  pl.BlockSpec((1, 1, 128), lambda h, i: (h, 0, 0)),
                  pl.BlockSpec((4, DA_HEAD_DIM), lambda h, i: (0, 0)),
                  pl.BlockSpec((1, DA_V_DIM), lambda h, i: (0, 0))],
        out_specs=pl.BlockSpec((tq, DA_V_DIM), lambda h, i: (i, h)),
        out_shape=jax.ShapeDtypeStruct((t, d_diff), BF16),
        scratch_shapes=[pltpu.VMEM((tq, DA_V_DIM), BF16),
                        pltpu.VMEM((3, tq, tk), F32),
                        pltpu.VMEM((8, 128), F32),
                        pltpu.VMEM((2, tq, 1), F32),
                        pltpu.VMEM((2, tq, 128), F32),
                        pltpu.VMEM((2 * tq, DA_V_DIM), F32)],
        compiler_params=_params("parallel", "arbitrary"),
    )(proj, proj, proj, slopes, lam_params, norm_g.reshape(1, DA_V_DIM))


def _up_conv_glu_kernel(h_ref, hp_ref, hn_ref, wg_ref, wv_ref, cwg_ref, cwv_ref, cbg_ref, cbv_ref,
                        o_ref, hext_ref):
    i, j = pl.program_id(0), pl.program_id(1)
    tm = h_ref.shape[0]
    ext = hext_ref.shape[0]

    @pl.when(j == 0)
    def _():
        hext_ref[HALO:HALO + tm, :] = h_ref[...]

        @pl.when(i > 0)
        def _():
            hext_ref[0:HALO, :] = hp_ref[...]

        @pl.when(i == 0)
        def _():
            hext_ref[0:HALO, :] = jnp.zeros_like(hp_ref)

        @pl.when(i < pl.num_programs(0) - 1)
        def _():
            hext_ref[HALO + tm:ext, :] = hn_ref[...]

        @pl.when(i == pl.num_programs(0) - 1)
        def _():
            hext_ref[HALO + tm:ext, :] = jnp.zeros_like(hn_ref)

    hext = hext_ref[...]

    def conv(w_ref, cw_ref, cb_ref):
        u = _dot(hext, w_ref[...])
        up = pltpu.roll(u, 1, axis=0)[HALO:HALO + tm]
        un = pltpu.roll(u, ext - 1, axis=0)[HALO:HALO + tm]
        cw = cw_ref[...]
        return up * cw[0:1] + u[HALO:HALO + tm] * cw[1:2] + un * cw[2:3] + cb_ref[...]

    gate = conv(wg_ref, cwg_ref, cbg_ref)
    val = conv(wv_ref, cwv_ref, cbv_ref)
    o_ref[...] = (gate * _sigmoid(gate) * val).astype(o_ref.dtype)


def _up_conv_glu(h, w_up, conv_w, conv_b, *, tm=1024, tn=512):
    t, d = h.shape
    n2 = w_up.shape[1]
    dff = n2 // 2
    tm, tn = min(tm, t), min(tn, dff)
    assert t % tm == 0 and dff % tn == 0 and tm % HALO == 0
    ncb = dff // tn
    rh = tm // HALO
    last = t // HALO - 1
    grid = (t // tm, ncb)

    def wcol(off):
        return pl.BlockSpec((d, tn), lambda i, j: (0, off + j))

    def prm(rows, off):
        return pl.BlockSpec((rows, tn), lambda i, j: (0, off + j))

    cb = conv_b.reshape(1, n2)
    return pl.pallas_call(
        _up_conv_glu_kernel,
        grid=grid,
        in_specs=[pl.BlockSpec((tm, d), lambda i, j: (i, 0)),
                  pl.BlockSpec((HALO, d), lambda i, j: (jnp.maximum(i * rh - 1, 0), 0)),
                  pl.BlockSpec((HALO, d), lambda i, j: (jnp.minimum((i + 1) * rh, last), 0)),
                  wcol(0), wcol(ncb),
                  prm(CONV_WIDTH, 0), prm(CONV_WIDTH, ncb), prm(1, 0), prm(1, ncb)],
        out_specs=pl.BlockSpec((tm, tn), lambda i, j: (i, j)),
        out_shape=jax.ShapeDtypeStruct((t, dff), BF16),
        scratch_shapes=[pltpu.VMEM((tm + 2 * HALO, d), BF16)],
        compiler_params=_params("parallel", "arbitrary"),
    )(h, h, h, w_up, w_up, conv_w, conv_w, cb, cb)


def kernel(x, c, w_ada, b_ada, ada_table, norm1_g, w_in, hg_lb_logits, hg_norm_g, da_lambda, da_norm_g,
           w_out, norm2_g, w_up, conv_w, conv_b, w_down, final_g):
    bsz, t, d = x.shape
    assert bsz == 1
    depth = w_in.shape[0]
    d_hgrn = hg_lb_logits.shape[-1]
    d_diff = w_out.shape[1] - d_hgrn
    assert w_in.shape[2] == 5 * d_hgrn + 3 * d_diff

    mod = _ada_mod(c, w_ada, b_ada, ada_table)
    wb_in, wb_out, wb_up, wb_down = (w[0].astype(BF16) for w in (w_in, w_out, w_up, w_down))
    xs = x.reshape(t, d)
    for l in range(depth):
        more = l + 1 < depth

        def cast(*ws):
            return [(w, l + 1) for w in ws] if more else []

        m = [mod[l, i * d:(i + 1) * d] for i in range(N_MOD)]
        h = _rmsnorm(xs, norm1_g[l], m[0], m[1])
        proj, nxt_in_up = _matmul([h], wb_in, tm=1024, tn=1024, cast=cast(w_in, w_up))
        o_fwd = _hgrn_pass(proj, hg_lb_logits, l, d_hgrn, reverse=False)
        o_hg = _hgrn_pass(proj, hg_lb_logits, l, d_hgrn, reverse=True, o_fwd=o_fwd, norm_g=hg_norm_g[l])
        o_da = _diff_attention(proj, da_lambda[l], da_norm_g[l], l, d_hgrn, d_diff)
        if d_hgrn == d_diff:
            mix = [o_hg, o_da]
        else:
            mix = [jnp.concatenate([o_hg, o_da], axis=1)]
        xs, nxt_out = _matmul(mix, wb_out, tm=1024, tn=512, x=xs, gate=m[2], cast=cast(w_out))
        h = _rmsnorm(xs, norm2_g[l], m[3], m[4])
        act = _up_conv_glu(h, wb_up, conv_w[l], conv_b[l])
        xs, nxt_down = _matmul([act], wb_down, tm=512, tn=512, x=xs, gate=m[5], cast=cast(w_down))
        if more:
            (wb_in, wb_up), (wb_out,), (wb_down,) = nxt_in_up, nxt_out, nxt_down
    out = _rmsnorm(xs, final_g, out_dtype=F32)
    return out.reshape(bsz, t, d)
```

```python
import functools
import math

import jax
import jax.numpy as jnp
import numpy as np
from jax import lax
from jax.experimental import pallas as pl
from jax.experimental.pallas import tpu as pltpu

F32 = jnp.float32
BF16 = jnp.bfloat16

HG_HEAD_DIM = 128
DA_HEAD_DIM = 128
DA_V_DIM = 2 * DA_HEAD_DIM
N_MOD = 6
CONV_WIDTH = 3
HALO = 16
EPS = 1e-6
HG_LEAF = 16
HG_LEAF_CLAMP = 80.0
DA_SKIP_MARGIN = 104.5
DA_FAST_RISE = 60.0
DA_FAST_GROUP = 3
VMEM_LIMIT_BYTES = 56 * 1024 * 1024


def _params(*semantics):
    return pltpu.CompilerParams(dimension_semantics=semantics, vmem_limit_bytes=VMEM_LIMIT_BYTES)


def _dot(a, b):
    return lax.dot_general(a, b, (((1,), (0,)), ((), ())), preferred_element_type=F32)


def _dot_nt(a, b):
    return lax.dot_general(a, b, (((1,), (1,)), ((), ())), preferred_element_type=F32)


def _dot_tn(a, b):
    return lax.dot_general(a, b, (((0,), (0,)), ((), ())), preferred_element_type=F32)


def _sigmoid(x):
    return 1.0 / (1.0 + jnp.exp(-x))


def _pick(n, *cands):
    for c in cands:
        if n % c == 0:
            return c
    return n


def _mod_kernel(c_ref, w_ref, b_ref, t_ref, o_ref):
    c = c_ref[...]
    s = c * _sigmoid(c)
    y = jnp.dot(s, w_ref[...], preferred_element_type=F32, precision=lax.Precision.HIGHEST)
    o_ref[...] = y[0:1, :] + b_ref[...] + t_ref[...]


def _ada_mod(c, w_ada, b_ada, ada_table):
    _, d = c.shape
    depth = ada_table.shape[0]
    n = w_ada.shape[1]
    tn = _pick(n, 512, 256, 128)
    c8 = jnp.broadcast_to(c, (8, d))
    return pl.pallas_call(
        _mod_kernel,
        grid=(n // tn,),
        in_specs=[pl.BlockSpec((8, d), lambda j: (0, 0)),
                  pl.BlockSpec((d, tn), lambda j: (0, j)),
                  pl.BlockSpec((1, tn), lambda j: (0, j)),
                  pl.BlockSpec((depth, tn), lambda j: (0, j))],
        out_specs=pl.BlockSpec((depth, tn), lambda j: (0, j)),
        out_shape=jax.ShapeDtypeStruct((depth, n), F32),
        compiler_params=_params("parallel"),
    )(c8, w_ada, b_ada.reshape(1, n), ada_table.reshape(depth, n))


def _norm_kernel(x_ref, g_ref, *rest, modulated):
    if modulated:
        shift_ref, scale_ref, o_ref = rest
    else:
        (o_ref,) = rest
    x = x_ref[...]
    y = x * lax.rsqrt(jnp.mean(x * x, axis=-1, keepdims=True) + EPS) * g_ref[...]
    if modulated:
        y = y * (1.0 + scale_ref[...]) + shift_ref[...]
    o_ref[...] = y.astype(o_ref.dtype)


def _rmsnorm(x, g, shift=None, scale=None, out_dtype=BF16):
    t, d = x.shape
    tr = _pick(t, 256, 128, 64, 32, 16, 8)
    modulated = shift is not None
    row = pl.BlockSpec((1, d), lambda i: (0, 0))
    args = [x, g.reshape(1, d)] + ([shift.reshape(1, d), scale.reshape(1, d)] if modulated else [])
    return pl.pallas_call(
        functools.partial(_norm_kernel, modulated=modulated),
        grid=(t // tr,),
        in_specs=[pl.BlockSpec((tr, d), lambda i: (i, 0))] + [row] * (len(args) - 1),
        out_specs=pl.BlockSpec((tr, d), lambda i: (i, 0)),
        out_shape=jax.ShapeDtypeStruct((t, d), out_dtype),
        compiler_params=_params("parallel"),
    )(*args)


def _cast_job(w_all, layer, grid):
    _, k, n = w_all.shape
    gi, gj = grid
    assert k % gi == 0 and n % gj == 0
    bk, bn = k // gi, n // gj
    assert bk % 16 == 0 and bn % 128 == 0
    return (pl.BlockSpec((None, bk, bn), lambda i, j: (layer, i, j)),
            pl.BlockSpec((bk, bn), lambda i, j: (i, j)),
            jax.ShapeDtypeStruct((k, n), BF16))


def _matmul_kernel(*refs, n_a, residual, n_cast):
    a_refs = refs[:n_a]
    w_ref = refs[n_a]
    n_in = n_a + 1 + (2 if residual else 0) + n_cast
    o_ref = refs[n_in]
    ka = a_refs[0].shape[1]
    acc = _dot(a_refs[0][...], w_ref[0:ka, :])
    for p in range(1, n_a):
        acc = acc + _dot(a_refs[p][...], w_ref[p * ka:(p + 1) * ka, :])
    if residual:
        x_ref, gate_ref = refs[n_a + 1], refs[n_a + 2]
        o_ref[...] = x_ref[...] + gate_ref[...] * acc
    else:
        o_ref[...] = acc.astype(o_ref.dtype)
    for c in range(n_cast):
        refs[n_in + 1 + c][...] = refs[n_in - n_cast + c][...].astype(BF16)


def _matmul(a_list, w, *, tm, tn, out_dtype=BF16, x=None, gate=None, cast=()):
    m, ka = a_list[0].shape
    n_a = len(a_list)
    kt, n = w.shape
    assert kt == n_a * ka and all(a.shape == (m, ka) for a in a_list)
    tm, tn = min(tm, m), min(tn, n)
    assert m % tm == 0 and n % tn == 0
    grid = (m // tm, n // tn)
    residual = x is not None
    in_specs = [pl.BlockSpec((tm, ka), lambda i, j: (i, 0)) for _ in range(n_a)]
    in_specs.append(pl.BlockSpec((kt, tn), lambda i, j: (0, j)))
    args = list(a_list) + [w]
    if residual:
        in_specs += [pl.BlockSpec((tm, tn), lambda i, j: (i, j)),
                     pl.BlockSpec((1, tn), lambda i, j: (0, j))]
        args += [x, gate.reshape(1, n)]
        out_dtype = F32
    out_specs = [pl.BlockSpec((tm, tn), lambda i, j: (i, j))]
    out_shape = [jax.ShapeDtypeStruct((m, n), out_dtype)]
    for w_all, layer in cast:
        c_in, c_out, c_shape = _cast_job(w_all, layer, grid)
        in_specs.append(c_in)
        args.append(w_all)
        out_specs.append(c_out)
        out_shape.append(c_shape)
    res = pl.pallas_call(
        functools.partial(_matmul_kernel, n_a=n_a, residual=residual, n_cast=len(cast)),
        grid=grid,
        in_specs=in_specs,
        out_specs=out_specs,
        out_shape=out_shape,
        compiler_params=_params("parallel", "parallel"),
    )(*args)
    return res[0], list(res[1:])


def _hgrn_level_ids(r, reverse):
    t = np.arange(r)[:, None]
    s = np.arange(r)[None, :]
    ids = np.full((r, r), -1, np.int32)
    tri = (s >= t) if reverse else (s <= t)
    ids[(t // HG_LEAF == s // HG_LEAF) & tri] = 0
    lv, i = HG_LEAF, 1
    while 2 * lv <= r:
        same = (t // (2 * lv)) == (s // (2 * lv))
        t_hi, s_hi = (t & lv) != 0, (s & lv) != 0
        pair = (~t_hi & s_hi) if reverse else (t_hi & ~s_hi)
        ids[same & pair] = i
        lv, i = 2 * lv, i + 1
    return ids


def _hgrn_kernel(q_ref, z_ref, v_ref, lbl_ref, ids_ref, *rest, reverse, layer, heads, final):
    if final:
        ofwd_ref, gate_ref, ng_ref, o_ref, st_ref = rest
    else:
        o_ref, st_ref = rest
    r = q_ref.shape[0]
    hd = HG_HEAD_DIM

    @pl.when(pl.program_id(1) == 0)
    def _():
        st_ref[...] = jnp.zeros_like(st_ref)

    logits = lbl_ref[0]
    e = jnp.exp(logits - jnp.max(logits, axis=0, keepdims=True))
    lb_all = (jnp.sum(e[1:layer + 1], axis=0, keepdims=True) / jnp.sum(e, axis=0, keepdims=True)
              if layer > 0 else jnp.zeros_like(e[0:1]))

    ids = ids_ref[...]
    n_levels = (r // HG_LEAF).bit_length()
    masks = [ids == level for level in range(n_levels)]
    row = lax.broadcasted_iota(jnp.int32, (r, r), 0)
    col = lax.broadcasted_iota(jnp.int32, (r, r), 1)
    tri = jnp.where((col >= row) if reverse else (col <= row), 1.0, 0.0).astype(BF16)
    total_row = 0 if reverse else r - 1

    for h in range(heads):
        sl = slice(h * hd, (h + 1) * hd)
        lb = lb_all[:, sl]
        q = q_ref[:, sl].astype(F32) * (hd ** -0.5)
        z = z_ref[:, sl].astype(F32)
        v = v_ref[:, sl]
        sig = _sigmoid(z)
        g = jnp.log(lb + (1.0 - lb) * sig)
        kc = (1.0 - lb) * (1.0 - sig)

        g1 = g.astype(BF16)
        rem = g - g1.astype(F32)
        g2 = rem.astype(BF16)
        g3 = (rem - g2.astype(F32)).astype(BF16)
        b = _dot(tri, g1) + _dot(tri, g2) + _dot(tri, g3)

        a = jnp.zeros((r, r), F32)
        lv, level = HG_LEAF // 2, 0
        while 2 * lv <= r:
            blk = 2 * lv
            nb = r // blk
            ref_row = lv if reverse else lv - 1
            bb = b.reshape(nb, blk, hd)
            d = bb - bb[:, ref_row:ref_row + 1, :]
            if level == 0:
                eq = jnp.exp(jnp.minimum(d, HG_LEAF_CLAMP))
                ek = jnp.exp(jnp.minimum(-d, HG_LEAF_CLAMP))
            else:
                eq = ek = jnp.exp(-jnp.abs(d))
            qe = (q.reshape(nb, blk, hd) * eq).reshape(r, hd).astype(BF16)
            ke = (kc.reshape(nb, blk, hd) * ek).reshape(r, hd).astype(BF16)
            a = jnp.where(masks[level], _dot_nt(qe, ke), a)
            lv, level = 2 * lv, level + 1
        assert level == n_levels

        st = st_ref[h]
        o = _dot(a.astype(BF16), v) + _dot_nt((q * jnp.exp(b)).astype(BF16), st.astype(BF16))
        bl = b[total_row:total_row + 1, :]
        kd = (kc * jnp.exp(bl - b)).astype(BF16)
        st_ref[h] = jnp.exp(bl) * st + _dot_tn(v, kd)

        if final:
            o = o + ofwd_ref[:, sl]
            y = o * lax.rsqrt(jnp.mean(o * o, axis=-1, keepdims=True) + EPS) * ng_ref[...]
            gt = gate_ref[:, sl].astype(F32)
            o_ref[:, sl] = (y * (gt * _sigmoid(gt))).astype(o_ref.dtype)
        else:
            o_ref[:, sl] = o


def _hgrn_pass(proj, lb_logits, layer, d_hgrn, *, reverse, o_fwd=None, norm_g=None, chunk=256):
    t = proj.shape[0]
    heads_total = d_hgrn // HG_HEAD_DIM
    hb = _pick(heads_total, 8, 4, 2, 1)
    w = hb * HG_HEAD_DIM
    r = min(chunk, t)
    assert t % r == 0 and r % (2 * HG_LEAF) == 0
    nchunks = t // r
    cpb = d_hgrn // w
    final = o_fwd is not None
    depth = lb_logits.shape[1]

    def rows(n):
        return (nchunks - 1 - n) if reverse else n

    def col_spec(group):
        return pl.BlockSpec((r, w), lambda h, n: (rows(n), group * cpb + h))

    in_specs = [col_spec(0), col_spec(2 if reverse else 1), col_spec(3),
                pl.BlockSpec((1, depth, w), lambda h, n: (1 if reverse else 0, 0, h)),
                pl.BlockSpec((r, r), lambda h, n: (0, 0))]
    args = [proj, proj, proj, lb_logits, jnp.asarray(_hgrn_level_ids(r, reverse))]
    if final:
        in_specs += [pl.BlockSpec((r, w), lambda h, n: (rows(n), h)), col_spec(4),
                     pl.BlockSpec((1, HG_HEAD_DIM), lambda h, n: (0, 0))]
        args += [o_fwd, proj, norm_g.reshape(1, HG_HEAD_DIM)]
    return pl.pallas_call(
        functools.partial(_hgrn_kernel, reverse=reverse, layer=layer, heads=hb, final=final),
        grid=(heads_total // hb, nchunks),
        in_specs=in_specs,
        out_specs=pl.BlockSpec((r, w), lambda h, n: (rows(n), h)),
        out_shape=jax.ShapeDtypeStruct((t, d_hgrn), BF16 if final else F32),
        scratch_shapes=[pltpu.VMEM((hb, HG_HEAD_DIM, HG_HEAD_DIM), F32)],
        compiler_params=_params("parallel", "arbitrary"),
    )(*args)


def _scaled_q(q_ref):
    return (q_ref[...].astype(F32) * (DA_HEAD_DIM ** -0.5)).astype(BF16)


def _max_sq_norm(x):
    sq = x.astype(F32)
    sq = sq * sq
    n = jnp.maximum(jnp.sum(sq[:, :DA_HEAD_DIM], axis=-1, keepdims=True),
                    jnp.sum(sq[:, DA_HEAD_DIM:], axis=-1, keepdims=True))
    return jnp.max(n, axis=0, keepdims=True)


def _split3(x):
    hi = x.astype(BF16).astype(F32)
    rem = x - hi
    mid = rem.astype(BF16).astype(F32)
    lo = (rem - mid).astype(BF16).astype(F32)
    return (hi, mid, lo)


def _lane_fields(shape, fields, first=0):
    lane = lax.broadcasted_iota(jnp.int32, shape, 1)
    out = jnp.zeros(shape, F32)
    for k, val in enumerate(fields):
        out = jnp.where(lane == first + k, val, out)
    return out


def _lane_block_sum(p):
    out = p[:, 0:128]
    for kb in range(1, p.shape[1] // 128):
        out = out + p[:, kb * 128:(kb + 1) * 128]
    return out


def _diff_attn_kernel(q_ref, k_ref, v_ref, slope_ref, lam_ref, ng_ref, o_ref,
                      qs_ref, bias_ref, kmax_ref, m_ref, l_ref, acc_ref, qx_ref, kx_ref, qxs_ref,
                      *, lam_init, tk, chunk):
    i = pl.program_id(1)
    tq = q_ref.shape[0]
    nkv = k_ref.shape[0] // tk
    hd = DA_HEAD_DIM
    c = slope_ref[0][0:1, 0:1]

    @pl.when(i == 0)
    def _():
        a = lax.broadcasted_iota(jnp.int32, (tq, tk), 0)
        b = lax.broadcasted_iota(jnp.int32, (tq, tk), 1)
        d = (a - b).astype(F32) * c
        bias_ref[0] = -d
        bias_ref[1] = d
        bias_ref[2] = -jnp.abs(d)
        rowk = lax.broadcasted_iota(jnp.int32, (tk, 1), 0).astype(F32) * c
        kx_ref[...] = _lane_fields((tk, 128), (1.0,) * 3 + _split3(rowk) + (1.0,) * 3).astype(BF16)
        rowq = lax.broadcasted_iota(jnp.int32, (tq, 1), 0).astype(F32) * c
        for side_idx, sgn in enumerate((1.0, -1.0)):
            qxs_ref[side_idx] = _lane_fields((tq, 128), _split3(-sgn * rowq) + (sgn,) * 3)

        def body(r, acc):
            rows = k_ref[pl.ds(pl.multiple_of(r * chunk, chunk), chunk), :]
            return jnp.maximum(acc, _max_sq_norm(rows))
        kmax = lax.fori_loop(0, k_ref.shape[0] // chunk, body, jnp.zeros((1, 1), F32))
        kmax_ref[...] = jnp.broadcast_to(kmax, kmax_ref.shape)

    qs = _scaled_q(q_ref)
    qs_ref[...] = qs
    bound = jnp.sqrt(_max_sq_norm(qs) * kmax_ref[0:1, 0:1])

    def tile(j):
        off = pl.multiple_of(j * tk, tk)
        cij = -c * jnp.abs(i * tq - j * tk).astype(F32)
        return k_ref[pl.ds(off, tk), :], v_ref[pl.ds(off, tk), :], cij

    def scores(mp, kt, bias):
        sl = slice(mp * hd, (mp + 1) * hd)
        return _dot_nt(qs_ref[:, sl], kt[:, sl]) + bias

    kt, vt, _ = tile(i)
    ps, m_min = [], None
    for mp in range(2):
        s = scores(mp, kt, bias_ref[2])
        m = jnp.max(s, axis=-1, keepdims=True)
        p = jnp.exp(s - m)
        m_ref[mp] = m
        l_ref[mp] = _lane_block_sum(p)
        ps.append(p.astype(BF16))
        m_low = jnp.min(m, axis=0, keepdims=True)
        m_min = m_low if m_min is None else jnp.minimum(m_min, m_low)
    acc_ref[...] = _dot(jnp.concatenate(ps, axis=0), vt)

    rise = bound - m_min
    w = jnp.floor((rise + DA_SKIP_MARGIN) / (c * tk)) + 1.0
    w = jnp.minimum(w, float(nkv)).astype(jnp.int32)[0, 0]
    fast = jnp.where(rise < DA_FAST_RISE, 1, 0)[0, 0]
    lo = jnp.maximum(i - w, 0)
    n_other = jnp.minimum(i + w, nkv - 1) - lo

    def other_tile(t):
        j = lo + t
        j = j + jnp.where(j >= i, 1, 0)
        side = jnp.where(j < i, 0, 1)
        return tile(j) + (side,)

    for mp in range(2):
        m_cols = _lane_fields((tq, 128), _split3(-m_ref[mp]), first=6)
        for side_idx in range(2):
            qx_ref[2 * side_idx + mp] = (qxs_ref[side_idx] + m_cols).astype(BF16)

    def fast_tiles(ts):
        probs, vals = [], []
        lsum = [None, None]
        for t in ts:
            kt, vt, cij, side = other_tile(t)
            ps = []
            for mp in range(2):
                sl = slice(mp * hd, (mp + 1) * hd)
                qcat = jnp.concatenate([qs_ref[:, sl], qx_ref[2 * side + mp]], axis=1)
                kcat = jnp.concatenate([kt[:, sl], kx_ref[...]], axis=1)
                p = jnp.exp(_dot_nt(qcat, kcat) + cij)
                part = _lane_block_sum(p)
                lsum[mp] = part if lsum[mp] is None else lsum[mp] + part
                ps.append(p.astype(BF16))
            probs.append(jnp.concatenate(ps, axis=0))
            vals.append(vt)
        for mp in range(2):
            l_ref[mp] += lsum[mp]
        acc_ref[...] += _dot(jnp.concatenate(probs, axis=1), jnp.concatenate(vals, axis=0))

    def fast_group(t, carry):
        fast_tiles(tuple(DA_FAST_GROUP * t + g for g in range(DA_FAST_GROUP)))
        return carry

    def fast_single(t, carry):
        fast_tiles((t,))
        return carry

    def safe_body(t, carry):
        kt, vt, cij, side = other_tile(t)
        bias = bias_ref[side]
        ps, alphas = [], []
        for mp in range(2):
            s = scores(mp, kt, bias)
            m_old = m_ref[mp]
            m_new = jnp.maximum(m_old, jnp.max(s, axis=-1, keepdims=True) + cij)
            alpha = jnp.exp(m_old - m_new)
            p = jnp.exp(s - (m_new - cij))
            l_ref[mp] = alpha * l_ref[mp] + _lane_block_sum(p)
            m_ref[mp] = m_new
            ps.append(p.astype(BF16))
            alphas.append(alpha)
        pv = _dot(jnp.concatenate(ps, axis=0), vt)
        for mp in range(2):
            rows = slice(mp * tq, (mp + 1) * tq)
            acc_ref[rows, :] = alphas[mp] * acc_ref[rows, :] + pv[rows]
        return carry

    n_fast = jnp.where(fast > 0, n_other, 0)
    lax.fori_loop(0, n_fast // DA_FAST_GROUP, fast_group, 0)
    lax.fori_loop(n_fast - n_fast % DA_FAST_GROUP, n_fast, fast_single, 0)
    lax.fori_loop(0, n_other - n_fast, safe_body, 0)

    lp = lam_ref[...]
    lam = (jnp.exp(jnp.sum(lp[0:1] * lp[1:2], axis=-1, keepdims=True))
           - jnp.exp(jnp.sum(lp[2:3] * lp[3:4], axis=-1, keepdims=True)) + lam_init)
    l0 = jnp.sum(l_ref[0], axis=-1, keepdims=True)
    l1 = jnp.sum(l_ref[1], axis=-1, keepdims=True)
    o = acc_ref[0:tq, :] / l0 - lam * (acc_ref[tq:2 * tq, :] / l1)
    y = o * lax.rsqrt(jnp.mean(o * o, axis=-1, keepdims=True) + EPS) * ng_ref[...]
    o_ref[...] = (y * (1.0 - lam_init)).astype(o_ref.dtype)


def _diff_attention(proj, lam_params, norm_g, layer, d_hgrn, d_diff, *, tile=512):
    t = proj.shape[0]
    heads = d_diff // DA_V_DIM
    tq = tk = min(tile, t)
    assert t % tq == 0 and tk % 128 == 0
    nq = t // tq
    lam_init = 0.8 - 0.6 * math.exp(-0.3 * layer)
    base = 5 * d_hgrn // DA_V_DIM
    slopes = np.exp2(-8.0 * np.arange(1, heads + 1, dtype=np.float32) / heads).astype(np.float32)
    slopes = jnp.asarray(np.broadcast_to(slopes[:, None, None], (heads, 1, 128)))

    return pl.pallas_call(
        functools.partial(_diff_attn_kernel, lam_init=lam_init, tk=tk, chunk=min(1024, t)),
        grid=(heads, nq),
        in_specs=[pl.BlockSpec((tq, DA_V_DIM), lambda h, i: (i, base + h)),
                  pl.BlockSpec((t, DA_V_DIM), lambda h, i: (0, base + heads + h)),
                  pl.BlockSpec((t, DA_V_DIM), lambda h, i: (0, base + 2 * heads + h)),
                  pl.BlockSpec((1, 1, 128), lambda h, i: (h, 0, 0)),
                  pl.BlockSpec((4, DA_HEAD_DIM), lambda h, i: (0, 0)),
                  pl.BlockSpec((1, DA_V_DIM), lambda h, i: (0, 0))],
        out_specs=pl.BlockSpec((tq, DA_V_DIM), lambda h, i: (i, h)),
        out_shape=jax.ShapeDtypeStruct((t, d_diff), BF16),
        scratch_shapes=[pltpu.VMEM((tq, DA_V_DIM), BF16),
                        pltpu.VMEM((3, tq, tk), F32),
                        pltpu.VMEM((8, 128), F32),
                        pltpu.VMEM((2, tq, 1), F32),
                        pltpu.VMEM((2, tq, 128), F32),
                        pltpu.VMEM((2 * tq, DA_V_DIM), F32),
                        pltpu.VMEM((4, tq, 128), BF16),
                        pltpu.VMEM((tk, 128), BF16),
                        pltpu.VMEM((2, tq, 128), F32)],
        compiler_params=_params("parallel", "arbitrary"),
    )(proj, proj, proj, slopes, lam_params, norm_g.reshape(1, DA_V_DIM))


def _up_conv_glu_kernel(h_ref, hp_ref, hn_ref, wg_ref, wv_ref, cwg_ref, cwv_ref, cbg_ref, cbv_ref,
                        o_ref, hext_ref):
    i, j = pl.program_id(0), pl.program_id(1)
    tm = h_ref.shape[0]
    ext = hext_ref.shape[0]

    @pl.when(j == 0)
    def _():
        hext_ref[HALO:HALO + tm, :] = h_ref[...]

        @pl.when(i > 0)
        def _():
            hext_ref[0:HALO, :] = hp_ref[...]

        @pl.when(i == 0)
        def _():
            hext_ref[0:HALO, :] = jnp.zeros_like(hp_ref)

        @pl.when(i < pl.num_programs(0) - 1)
        def _():
            hext_ref[HALO + tm:ext, :] = hn_ref[...]

        @pl.when(i == pl.num_programs(0) - 1)
        def _():
            hext_ref[HALO + tm:ext, :] = jnp.zeros_like(hn_ref)

    hext = hext_ref[...]

    def conv(w_ref, cw_ref, cb_ref):
        u = _dot(hext, w_ref[...])
        up = pltpu.roll(u, 1, axis=0)[HALO:HALO + tm]
        un = pltpu.roll(u, ext - 1, axis=0)[HALO:HALO + tm]
        cw = cw_ref[...]
        return up * cw[0:1] + u[HALO:HALO + tm] * cw[1:2] + un * cw[2:3] + cb_ref[...]

    gate = conv(wg_ref, cwg_ref, cbg_ref)
    val = conv(wv_ref, cwv_ref, cbv_ref)
    o_ref[...] = (gate * _sigmoid(gate) * val).astype(o_ref.dtype)


def _up_conv_glu(h, w_up, conv_w, conv_b, *, tm=1024, tn=512):
    t, d = h.shape
    n2 = w_up.shape[1]
    dff = n2 // 2
    tm, tn = min(tm, t), min(tn, dff)
    assert t % tm == 0 and dff % tn == 0 and tm % HALO == 0
    ncb = dff // tn
    rh = tm // HALO
    last = t // HALO - 1
    grid = (t // tm, ncb)

    def wcol(off):
        return pl.BlockSpec((d, tn), lambda i, j: (0, off + j))

    def prm(rows, off):
        return pl.BlockSpec((rows, tn), lambda i, j: (0, off + j))

    cb = conv_b.reshape(1, n2)
    return pl.pallas_call(
        _up_conv_glu_kernel,
        grid=grid,
        in_specs=[pl.BlockSpec((tm, d), lambda i, j: (i, 0)),
                  pl.BlockSpec((HALO, d), lambda i, j: (jnp.maximum(i * rh - 1, 0), 0)),
                  pl.BlockSpec((HALO, d), lambda i, j: (jnp.minimum((i + 1) * rh, last), 0)),
                  wcol(0), wcol(ncb),
                  prm(CONV_WIDTH, 0), prm(CONV_WIDTH, ncb), prm(1, 0), prm(1, ncb)],
        out_specs=pl.BlockSpec((tm, tn), lambda i, j: (i, j)),
        out_shape=jax.ShapeDtypeStruct((t, dff), BF16),
        scratch_shapes=[pltpu.VMEM((tm + 2 * HALO, d), BF16)],
        compiler_params=_params("parallel", "arbitrary"),
    )(h, h, h, w_up, w_up, conv_w, conv_w, cb, cb)


def kernel(x, c, w_ada, b_ada, ada_table, norm1_g, w_in, hg_lb_logits, hg_norm_g, da_lambda, da_norm_g,
           w_out, norm2_g, w_up, conv_w, conv_b, w_down, final_g):
    bsz, t, d = x.shape
    assert bsz == 1
    depth = w_in.shape[0]
    d_hgrn = hg_lb_logits.shape[-1]
    d_diff = w_out.shape[1] - d_hgrn
    assert w_in.shape[2] == 5 * d_hgrn + 3 * d_diff

    mod = _ada_mod(c, w_ada, b_ada, ada_table)
    wb_in, wb_out, wb_up, wb_down = (w[0].astype(BF16) for w in (w_in, w_out, w_up, w_down))
    xs = x.reshape(t, d)
    for l in range(depth):
        more = l + 1 < depth

        def cast(*ws):
            return [(w, l + 1) for w in ws] if more else []

        m = [mod[l, i * d:(i + 1) * d] for i in range(N_MOD)]
        h = _rmsnorm(xs, norm1_g[l], m[0], m[1])
        proj, nxt_in_up = _matmul([h], wb_in, tm=1024, tn=1024, cast=cast(w_in, w_up))
        o_fwd = _hgrn_pass(proj, hg_lb_logits, l, d_hgrn, reverse=False)
        o_hg = _hgrn_pass(proj, hg_lb_logits, l, d_hgrn, reverse=True, o_fwd=o_fwd, norm_g=hg_norm_g[l])
        o_da = _diff_attention(proj, da_lambda[l], da_norm_g[l], l, d_hgrn, d_diff)
        if d_hgrn == d_diff:
            mix = [o_hg, o_da]
        else:
            mix = [jnp.concatenate([o_hg, o_da], axis=1)]
        xs, nxt_out = _matmul(mix, wb_out, tm=1024, tn=512, x=xs, gate=m[2], cast=cast(w_out))
        h = _rmsnorm(xs, norm2_g[l], m[3], m[4])
        act = _up_conv_glu(h, wb_up, conv_w[l], conv_b[l])
        xs, nxt_down = _matmul([act], wb_down, tm=512, tn=512, x=xs, gate=m[5], cast=cast(w_down))
        if more:
            (wb_in, wb_up), (wb_out,), (wb_down,) = nxt_in_up, nxt_out, nxt_down
    out = _rmsnorm(xs, final_g, out_dtype=F32)
    return out.reshape(bsz, t, d)
```

```python
import functools
import math

import jax
import jax.numpy as jnp
import numpy as np
from jax import lax
from jax.experimental import pallas as pl
from jax.experimental.pallas import tpu as pltpu

F32 = jnp.float32
BF16 = jnp.bfloat16

HG_HEAD_DIM = 128
DA_HEAD_DIM = 128
DA_V_DIM = 2 * DA_HEAD_DIM
N_MOD = 6
CONV_WIDTH = 3
HALO = 16
EPS = 1e-6
HG_LEAF = 16
HG_LEAF_CLAMP = 80.0
DA_SKIP_MARGIN = 104.5
DA_FAST_RISE = 60.0
DA_FAST_GROUPS = (6, 2, 1)
VMEM_LIMIT_BYTES = 56 * 1024 * 1024


def _params(*semantics):
    return pltpu.CompilerParams(dimension_semantics=semantics, vmem_limit_bytes=VMEM_LIMIT_BYTES)


def _dot(a, b):
    return lax.dot_general(a, b, (((1,), (0,)), ((), ())), preferred_element_type=F32)


def _dot_nt(a, b):
    return lax.dot_general(a, b, (((1,), (1,)), ((), ())), preferred_element_type=F32)


def _dot_tn(a, b):
    return lax.dot_general(a, b, (((0,), (0,)), ((), ())), preferred_element_type=F32)


def _sigmoid(x):
    return 1.0 / (1.0 + jnp.exp(-x))


def _pick(n, *cands):
    for c in cands:
        if n % c == 0:
            return c
    return n


def _mod_kernel(c_ref, w_ref, b_ref, t_ref, o_ref):
    c = c_ref[...]
    s = c * _sigmoid(c)
    y = jnp.dot(s, w_ref[...], preferred_element_type=F32, precision=lax.Precision.HIGHEST)
    o_ref[...] = y[0:1, :] + b_ref[...] + t_ref[...]


def _ada_mod(c, w_ada, b_ada, ada_table):
    _, d = c.shape
    depth = ada_table.shape[0]
    n = w_ada.shape[1]
    tn = _pick(n, 512, 256, 128)
    c8 = jnp.broadcast_to(c, (8, d))
    return pl.pallas_call(
        _mod_kernel,
        grid=(n // tn,),
        in_specs=[pl.BlockSpec((8, d), lambda j: (0, 0)),
                  pl.BlockSpec((d, tn), lambda j: (0, j)),
                  pl.BlockSpec((1, tn), lambda j: (0, j)),
                  pl.BlockSpec((depth, tn), lambda j: (0, j))],
        out_specs=pl.BlockSpec((depth, tn), lambda j: (0, j)),
        out_shape=jax.ShapeDtypeStruct((depth, n), F32),
        compiler_params=_params("parallel"),
    )(c8, w_ada, b_ada.reshape(1, n), ada_table.reshape(depth, n))


def _norm_kernel(x_ref, g_ref, *rest, modulated):
    if modulated:
        shift_ref, scale_ref, o_ref = rest
    else:
        (o_ref,) = rest
    x = x_ref[...]
    y = x * lax.rsqrt(jnp.mean(x * x, axis=-1, keepdims=True) + EPS) * g_ref[...]
    if modulated:
        y = y * (1.0 + scale_ref[...]) + shift_ref[...]
    o_ref[...] = y.astype(o_ref.dtype)


def _rmsnorm(x, g, shift=None, scale=None, out_dtype=BF16):
    t, d = x.shape
    tr = _pick(t, 256, 128, 64, 32, 16, 8)
    modulated = shift is not None
    row = pl.BlockSpec((1, d), lambda i: (0, 0))
    args = [x, g.reshape(1, d)] + ([shift.reshape(1, d), scale.reshape(1, d)] if modulated else [])
    return pl.pallas_call(
        functools.partial(_norm_kernel, modulated=modulated),
        grid=(t // tr,),
        in_specs=[pl.BlockSpec((tr, d), lambda i: (i, 0))] + [row] * (len(args) - 1),
        out_specs=pl.BlockSpec((tr, d), lambda i: (i, 0)),
        out_shape=jax.ShapeDtypeStruct((t, d), out_dtype),
        compiler_params=_params("parallel"),
    )(*args)


def _cast_job(w_all, layer, grid):
    _, k, n = w_all.shape
    gi, gj = grid
    assert k % gi == 0 and n % gj == 0
    bk, bn = k // gi, n // gj
    assert bk % 16 == 0 and bn % 128 == 0
    return (pl.BlockSpec((None, bk, bn), lambda i, j: (layer, i, j)),
            pl.BlockSpec((bk, bn), lambda i, j: (i, j)),
            jax.ShapeDtypeStruct((k, n), BF16))


def _matmul_kernel(*refs, n_a, residual, n_cast):
    a_refs = refs[:n_a]
    w_ref = refs[n_a]
    n_in = n_a + 1 + (2 if residual else 0) + n_cast
    o_ref = refs[n_in]
    ka = a_refs[0].shape[1]
    acc = _dot(a_refs[0][...], w_ref[0:ka, :])
    for p in range(1, n_a):
        acc = acc + _dot(a_refs[p][...], w_ref[p * ka:(p + 1) * ka, :])
    if residual:
        x_ref, gate_ref = refs[n_a + 1], refs[n_a + 2]
        o_ref[...] = x_ref[...] + gate_ref[...] * acc
    else:
        o_ref[...] = acc.astype(o_ref.dtype)
    for c in range(n_cast):
        refs[n_in + 1 + c][...] = refs[n_in - n_cast + c][...].astype(BF16)


def _matmul(a_list, w, *, tm, tn, out_dtype=BF16, x=None, gate=None, cast=()):
    m, ka = a_list[0].shape
    n_a = len(a_list)
    kt, n = w.shape
    assert kt == n_a * ka and all(a.shape == (m, ka) for a in a_list)
    tm, tn = min(tm, m), min(tn, n)
    assert m % tm == 0 and n % tn == 0
    grid = (m // tm, n // tn)
    residual = x is not None
    in_specs = [pl.BlockSpec((tm, ka), lambda i, j: (i, 0)) for _ in range(n_a)]
    in_specs.append(pl.BlockSpec((kt, tn), lambda i, j: (0, j)))
    args = list(a_list) + [w]
    if residual:
        in_specs += [pl.BlockSpec((tm, tn), lambda i, j: (i, j)),
                     pl.BlockSpec((1, tn), lambda i, j: (0, j))]
        args += [x, gate.reshape(1, n)]
        out_dtype = F32
    out_specs = [pl.BlockSpec((tm, tn), lambda i, j: (i, j))]
    out_shape = [jax.ShapeDtypeStruct((m, n), out_dtype)]
    for w_all, layer in cast:
        c_in, c_out, c_shape = _cast_job(w_all, layer, grid)
        in_specs.append(c_in)
        args.append(w_all)
        out_specs.append(c_out)
        out_shape.append(c_shape)
    res = pl.pallas_call(
        functools.partial(_matmul_kernel, n_a=n_a, residual=residual, n_cast=len(cast)),
        grid=grid,
        in_specs=in_specs,
        out_specs=out_specs,
        out_shape=out_shape,
        compiler_params=_params("parallel", "parallel"),
    )(*args)
    return res[0], list(res[1:])


def _hgrn_level_ids(r, reverse):
    t = np.arange(r)[:, None]
    s = np.arange(r)[None, :]
    ids = np.full((r, r), -1, np.int32)
    tri = (s >= t) if reverse else (s <= t)
    ids[(t // HG_LEAF == s // HG_LEAF) & tri] = 0
    lv, i = HG_LEAF, 1
    while 2 * lv <= r:
        same = (t // (2 * lv)) == (s // (2 * lv))
        t_hi, s_hi = (t & lv) != 0, (s & lv) != 0
        pair = (~t_hi & s_hi) if reverse else (t_hi & ~s_hi)
        ids[same & pair] = i
        lv, i = 2 * lv, i + 1
    return ids


def _hgrn_kernel(q_ref, z_ref, v_ref, lbl_ref, ids_ref, *rest, reverse, layer, heads, final):
    if final:
        ofwd_ref, gate_ref, ng_ref, o_ref, st_ref = rest
    else:
        o_ref, st_ref = rest
    r = q_ref.shape[0]
    hd = HG_HEAD_DIM

    @pl.when(pl.program_id(1) == 0)
    def _():
        st_ref[...] = jnp.zeros_like(st_ref)

    logits = lbl_ref[0]
    e = jnp.exp(logits - jnp.max(logits, axis=0, keepdims=True))
    lb_all = (jnp.sum(e[1:layer + 1], axis=0, keepdims=True) / jnp.sum(e, axis=0, keepdims=True)
              if layer > 0 else jnp.zeros_like(e[0:1]))

    ids = ids_ref[...]
    n_levels = (r // HG_LEAF).bit_length()
    masks = [ids == level for level in range(n_levels)]
    row = lax.broadcasted_iota(jnp.int32, (r, r), 0)
    col = lax.broadcasted_iota(jnp.int32, (r, r), 1)
    tri = jnp.where((col >= row) if reverse else (col <= row), 1.0, 0.0).astype(BF16)
    total_row = 0 if reverse else r - 1

    for h in range(heads):
        sl = slice(h * hd, (h + 1) * hd)
        lb = lb_all[:, sl]
        q = q_ref[:, sl].astype(F32) * (hd ** -0.5)
        z = z_ref[:, sl].astype(F32)
        v = v_ref[:, sl]
        sig = _sigmoid(z)
        g = jnp.log(lb + (1.0 - lb) * sig)
        kc = (1.0 - lb) * (1.0 - sig)

        g1 = g.astype(BF16)
        rem = g - g1.astype(F32)
        g2 = rem.astype(BF16)
        g3 = (rem - g2.astype(F32)).astype(BF16)
        b = _dot(tri, g1) + _dot(tri, g2) + _dot(tri, g3)

        a = jnp.zeros((r, r), F32)
        lv, level = HG_LEAF // 2, 0
        while 2 * lv <= r:
            blk = 2 * lv
            nb = r // blk
            ref_row = lv if reverse else lv - 1
            bb = b.reshape(nb, blk, hd)
            d = bb - bb[:, ref_row:ref_row + 1, :]
            if level == 0:
                eq = jnp.exp(jnp.minimum(d, HG_LEAF_CLAMP))
                ek = jnp.exp(jnp.minimum(-d, HG_LEAF_CLAMP))
            else:
                eq = ek = jnp.exp(-jnp.abs(d))
            qe = (q.reshape(nb, blk, hd) * eq).reshape(r, hd).astype(BF16)
            ke = (kc.reshape(nb, blk, hd) * ek).reshape(r, hd).astype(BF16)
            a = jnp.where(masks[level], _dot_nt(qe, ke), a)
            lv, level = 2 * lv, level + 1
        assert level == n_levels

        st = st_ref[h]
        o = _dot(a.astype(BF16), v) + _dot_nt((q * jnp.exp(b)).astype(BF16), st.astype(BF16))
        bl = b[total_row:total_row + 1, :]
        kd = (kc * jnp.exp(bl - b)).astype(BF16)
        st_ref[h] = jnp.exp(bl) * st + _dot_tn(v, kd)

        if final:
            o = o + ofwd_ref[:, sl]
            y = o * lax.rsqrt(jnp.mean(o * o, axis=-1, keepdims=True) + EPS) * ng_ref[...]
            gt = gate_ref[:, sl].astype(F32)
            o_ref[:, sl] = (y * (gt * _sigmoid(gt))).astype(o_ref.dtype)
        else:
            o_ref[:, sl] = o


def _hgrn_pass(proj, lb_logits, layer, d_hgrn, *, reverse, o_fwd=None, norm_g=None, chunk=256):
    t = proj.shape[0]
    heads_total = d_hgrn // HG_HEAD_DIM
    hb = _pick(heads_total, 8, 4, 2, 1)
    w = hb * HG_HEAD_DIM
    r = min(chunk, t)
    assert t % r == 0 and r % (2 * HG_LEAF) == 0
    nchunks = t // r
    cpb = d_hgrn // w
    final = o_fwd is not None
    depth = lb_logits.shape[1]

    def rows(n):
        return (nchunks - 1 - n) if reverse else n

    def col_spec(group):
        return pl.BlockSpec((r, w), lambda h, n: (rows(n), group * cpb + h))

    in_specs = [col_spec(0), col_spec(2 if reverse else 1), col_spec(3),
                pl.BlockSpec((1, depth, w), lambda h, n: (1 if reverse else 0, 0, h)),
                pl.BlockSpec((r, r), lambda h, n: (0, 0))]
    args = [proj, proj, proj, lb_logits, jnp.asarray(_hgrn_level_ids(r, reverse))]
    if final:
        in_specs += [pl.BlockSpec((r, w), lambda h, n: (rows(n), h)), col_spec(4),
                     pl.BlockSpec((1, HG_HEAD_DIM), lambda h, n: (0, 0))]
        args += [o_fwd, proj, norm_g.reshape(1, HG_HEAD_DIM)]
    return pl.pallas_call(
        functools.partial(_hgrn_kernel, reverse=reverse, layer=layer, heads=hb, final=final),
        grid=(heads_total // hb, nchunks),
        in_specs=in_specs,
        out_specs=pl.BlockSpec((r, w), lambda h, n: (rows(n), h)),
        out_shape=jax.ShapeDtypeStruct((t, d_hgrn), BF16 if final else F32),
        scratch_shapes=[pltpu.VMEM((hb, HG_HEAD_DIM, HG_HEAD_DIM), F32)],
        compiler_params=_params("parallel", "arbitrary"),
    )(*args)


def _scaled_q(q_ref):
    return (q_ref[...].astype(F32) * (DA_HEAD_DIM ** -0.5)).astype(BF16)


def _max_sq_norm(x):
    sq = x.astype(F32)
    sq = sq * sq
    n = jnp.maximum(jnp.sum(sq[:, :DA_HEAD_DIM], axis=-1, keepdims=True),
                    jnp.sum(sq[:, DA_HEAD_DIM:], axis=-1, keepdims=True))
    return jnp.max(n, axis=0, keepdims=True)


def _split3(x):
    hi = x.astype(BF16).astype(F32)
    rem = x - hi
    mid = rem.astype(BF16).astype(F32)
    lo = (rem - mid).astype(BF16).astype(F32)
    return (hi, mid, lo)


def _lane_fields(shape, fields, first=0):
    lane = lax.broadcasted_iota(jnp.int32, shape, 1)
    out = jnp.zeros(shape, F32)
    for k, val in enumerate(fields):
        out = jnp.where(lane == first + k, val, out)
    return out


def _lane_block_sum(p):
    out = p[:, 0:128]
    for kb in range(1, p.shape[1] // 128):
        out = out + p[:, kb * 128:(kb + 1) * 128]
    return out


def _diff_attn_kernel(q_ref, k_ref, v_ref, slope_ref, lam_ref, ng_ref, o_ref,
                      qs_ref, bias_ref, kmax_ref, m_ref, l_ref, acc_ref, qx_ref, kx_ref, qxs_ref,
                      *, lam_init, tk, chunk):
    i = pl.program_id(1)
    tq = q_ref.shape[0]
    nkv = k_ref.shape[0] // tk
    hd = DA_HEAD_DIM
    c = slope_ref[0][0:1, 0:1]

    @pl.when(i == 0)
    def _():
        a = lax.broadcasted_iota(jnp.int32, (tq, tk), 0)
        b = lax.broadcasted_iota(jnp.int32, (tq, tk), 1)
        d = (a - b).astype(F32) * c
        bias_ref[0] = -d
        bias_ref[1] = d
        bias_ref[2] = -jnp.abs(d)
        rowk = lax.broadcasted_iota(jnp.int32, (tk, 1), 0).astype(F32) * c
        kx_ref[...] = _lane_fields((tk, 128), (1.0,) * 3 + _split3(rowk) + (1.0,)).astype(BF16)
        rowq = lax.broadcasted_iota(jnp.int32, (tq, 1), 0).astype(F32) * c
        for side_idx, sgn in enumerate((1.0, -1.0)):
            qxs_ref[side_idx] = _lane_fields((tq, 128), _split3(-sgn * rowq) + (sgn,) * 3)

        def body(r, acc):
            rows = k_ref[pl.ds(pl.multiple_of(r * chunk, chunk), chunk), :]
            return jnp.maximum(acc, _max_sq_norm(rows))
        kmax = lax.fori_loop(0, k_ref.shape[0] // chunk, body, jnp.zeros((1, 1), F32))
        kmax_ref[...] = jnp.broadcast_to(kmax, kmax_ref.shape)

    qs = _scaled_q(q_ref)
    qs_ref[...] = qs
    bound = jnp.sqrt(_max_sq_norm(qs) * kmax_ref[0:1, 0:1])

    def tile(j):
        off = pl.multiple_of(j * tk, tk)
        cij = -c * jnp.abs(i * tq - j * tk).astype(F32)
        return k_ref[pl.ds(off, tk), :], v_ref[pl.ds(off, tk), :], cij

    def scores(mp, kt, bias):
        sl = slice(mp * hd, (mp + 1) * hd)
        return _dot_nt(qs_ref[:, sl], kt[:, sl]) + bias

    kt, vt, _ = tile(i)
    ps, m_min = [], None
    for mp in range(2):
        s = scores(mp, kt, bias_ref[2])
        m = jnp.max(s, axis=-1, keepdims=True).astype(BF16).astype(F32)
        p = jnp.exp(s - m)
        m_ref[mp] = m
        l_ref[mp] = _lane_block_sum(p)
        ps.append(p.astype(BF16))
        m_low = jnp.min(m, axis=0, keepdims=True)
        m_min = m_low if m_min is None else jnp.minimum(m_min, m_low)
    acc_ref[...] = _dot(jnp.concatenate(ps, axis=0), vt)

    rise = bound - m_min
    w = jnp.floor((rise + DA_SKIP_MARGIN) / (c * tk)) + 1.0
    w = jnp.minimum(w, float(nkv)).astype(jnp.int32)[0, 0]
    fast = jnp.where(rise < DA_FAST_RISE, 1, 0)[0, 0]
    lo = jnp.maximum(i - w, 0)
    n_other = jnp.minimum(i + w, nkv - 1) - lo

    def other_tile(t):
        j = lo + t
        j = j + jnp.where(j >= i, 1, 0)
        side = jnp.where(j < i, 0, 1)
        return tile(j) + (side,)

    for mp in range(2):
        m_cols = _lane_fields((tq, 128), (-m_ref[mp],), first=6)
        for side_idx in range(2):
            qx_ref[2 * side_idx + mp] = (qxs_ref[side_idx] + m_cols).astype(BF16)

    def fast_tiles(ts):
        probs, vals = [], []
        lsum = [None, None]
        for t in ts:
            kt, vt, cij, side = other_tile(t)
            ps = []
            for mp in range(2):
                sl = slice(mp * hd, (mp + 1) * hd)
                qcat = jnp.concatenate([qs_ref[:, sl], qx_ref[2 * side + mp]], axis=1)
                kcat = jnp.concatenate([kt[:, sl], kx_ref[...]], axis=1)
                p = jnp.exp(_dot_nt(qcat, kcat) + cij)
                part = _lane_block_sum(p)
                lsum[mp] = part if lsum[mp] is None else lsum[mp] + part
                ps.append(p.astype(BF16))
            probs.append(jnp.concatenate(ps, axis=0))
            vals.append(vt)
        for mp in range(2):
            l_ref[mp] += lsum[mp]
        acc_ref[...] += _dot(jnp.concatenate(probs, axis=1), jnp.concatenate(vals, axis=0))


    def safe_body(t, carry):
        kt, vt, cij, side = other_tile(t)
        bias = bias_ref[side]
        ps, alphas = [], []
        for mp in range(2):
            s = scores(mp, kt, bias)
            m_old = m_ref[mp]
            m_new = jnp.maximum(m_old, jnp.max(s, axis=-1, keepdims=True) + cij)
            alpha = jnp.exp(m_old - m_new)
            p = jnp.exp(s - (m_new - cij))
            l_ref[mp] = alpha * l_ref[mp] + _lane_block_sum(p)
            m_ref[mp] = m_new
            ps.append(p.astype(BF16))
            alphas.append(alpha)
        pv = _dot(jnp.concatenate(ps, axis=0), vt)
        for mp in range(2):
            rows = slice(mp * tq, (mp + 1) * tq)
            acc_ref[rows, :] = alphas[mp] * acc_ref[rows, :] + pv[rows]
        return carry

    n_fast = jnp.where(fast > 0, n_other, 0)
    start = 0
    for group in DA_FAST_GROUPS:
        count = (n_fast - start) // group

        def fast_group(t, carry, start=start, group=group):
            fast_tiles(tuple(start + group * t + g for g in range(group)))
            return carry

        lax.fori_loop(0, count, fast_group, 0)
        start = start + count * group
    lax.fori_loop(0, n_other - n_fast, safe_body, 0)

    lp = lam_ref[...]
    lam = (jnp.exp(jnp.sum(lp[0:1] * lp[1:2], axis=-1, keepdims=True))
           - jnp.exp(jnp.sum(lp[2:3] * lp[3:4], axis=-1, keepdims=True)) + lam_init)
    l0 = jnp.sum(l_ref[0], axis=-1, keepdims=True)
    l1 = jnp.sum(l_ref[1], axis=-1, keepdims=True)
    o = acc_ref[0:tq, :] / l0 - lam * (acc_ref[tq:2 * tq, :] / l1)
    y = o * lax.rsqrt(jnp.mean(o * o, axis=-1, keepdims=True) + EPS) * ng_ref[...]
    o_ref[...] = (y * (1.0 - lam_init)).astype(o_ref.dtype)


def _diff_attention(proj, lam_params, norm_g, layer, d_hgrn, d_diff, *, tile=512):
    t = proj.shape[0]
    heads = d_diff // DA_V_DIM
    tq = tk = min(tile, t)
    assert t % tq == 0 and tk % 128 == 0
    nq = t // tq
    lam_init = 0.8 - 0.6 * math.exp(-0.3 * layer)
    base = 5 * d_hgrn // DA_V_DIM
    slopes = np.exp2(-8.0 * np.arange(1, heads + 1, dtype=np.float32) / heads).astype(np.float32)
    slopes = jnp.asarray(np.broadcast_to(slopes[:, None, None], (heads, 1, 128)))

    return pl.pallas_call(
        functools.partial(_diff_attn_kernel, lam_init=lam_init, tk=tk, chunk=min(1024, t)),
        grid=(heads, nq),
        in_specs=[pl.BlockSpec((tq, DA_V_DIM), lambda h, i: (i, base + h)),
                  pl.BlockSpec((t, DA_V_DIM), lambda h, i: (0, base + heads + h)),
                  pl.BlockSpec((t, DA_V_DIM), lambda h, i: (0, base + 2 * heads + h)),
                  pl.BlockSpec((1, 1, 128), lambda h, i: (h, 0, 0)),
                  pl.BlockSpec((4, DA_HEAD_DIM), lambda h, i: (0, 0)),
                  pl.BlockSpec((1, DA_V_DIM), lambda h, i: (0, 0))],
        out_specs=pl.BlockSpec((tq, DA_V_DIM), lambda h, i: (i, h)),
        out_shape=jax.ShapeDtypeStruct((t, d_diff), BF16),
        scratch_shapes=[pltpu.VMEM((tq, DA_V_DIM), BF16),
                        pltpu.VMEM((3, tq, tk), F32),
                        pltpu.VMEM((8, 128), F32),
                        pltpu.VMEM((2, tq, 1), F32),
                        pltpu.VMEM((2, tq, 128), F32),
                        pltpu.VMEM((2 * tq, DA_V_DIM), F32),
                        pltpu.VMEM((4, tq, 128), BF16),
                        pltpu.VMEM((tk, 128), BF16),
                        pltpu.VMEM((2, tq, 128), F32)],
        compiler_params=_params("parallel", "arbitrary"),
    )(proj, proj, proj, slopes, lam_params, norm_g.reshape(1, DA_V_DIM))


def _up_conv_glu_kernel(h_ref, hp_ref, hn_ref, wg_ref, wv_ref, cwg_ref, cwv_ref, cbg_ref, cbv_ref,
                        o_ref, hext_ref):
    i, j = pl.program_id(0), pl.program_id(1)
    tm = h_ref.shape[0]
    ext = hext_ref.shape[0]

    @pl.when(j == 0)
    def _():
        hext_ref[HALO:HALO + tm, :] = h_ref[...]

        @pl.when(i > 0)
        def _():
            hext_ref[0:HALO, :] = hp_ref[...]

        @pl.when(i == 0)
        def _():
            hext_ref[0:HALO, :] = jnp.zeros_like(hp_ref)

        @pl.when(i < pl.num_programs(0) - 1)
        def _():
            hext_ref[HALO + tm:ext, :] = hn_ref[...]

        @pl.when(i == pl.num_programs(0) - 1)
        def _():
            hext_ref[HALO + tm:ext, :] = jnp.zeros_like(hn_ref)

    hext = hext_ref[...]

    def conv(w_ref, cw_ref, cb_ref):
        u = _dot(hext, w_ref[...])
        up = pltpu.roll(u, 1, axis=0)[HALO:HALO + tm]
        un = pltpu.roll(u, ext - 1, axis=0)[HALO:HALO + tm]
        cw = cw_ref[...]
        return up * cw[0:1] + u[HALO:HALO + tm] * cw[1:2] + un * cw[2:3] + cb_ref[...]

    gate = conv(wg_ref, cwg_ref, cbg_ref)
    val = conv(wv_ref, cwv_ref, cbv_ref)
    o_ref[...] = (gate * _sigmoid(gate) * val).astype(o_ref.dtype)


def _up_conv_glu(h, w_up, conv_w, conv_b, *, tm=1024, tn=512):
    t, d = h.shape
    n2 = w_up.shape[1]
    dff = n2 // 2
    tm, tn = min(tm, t), min(tn, dff)
    assert t % tm == 0 and dff % tn == 0 and tm % HALO == 0
    ncb = dff // tn
    rh = tm // HALO
    last = t // HALO - 1
    grid = (t // tm, ncb)

    def wcol(off):
        return pl.BlockSpec((d, tn), lambda i, j: (0, off + j))

    def prm(rows, off):
        return pl.BlockSpec((rows, tn), lambda i, j: (0, off + j))

    cb = conv_b.reshape(1, n2)
    return pl.pallas_call(
        _up_conv_glu_kernel,
        grid=grid,
        in_specs=[pl.BlockSpec((tm, d), lambda i, j: (i, 0)),
                  pl.BlockSpec((HALO, d), lambda i, j: (jnp.maximum(i * rh - 1, 0), 0)),
                  pl.BlockSpec((HALO, d), lambda i, j: (jnp.minimum((i + 1) * rh, last), 0)),
                  wcol(0), wcol(ncb),
                  prm(CONV_WIDTH, 0), prm(CONV_WIDTH, ncb), prm(1, 0), prm(1, ncb)],
        out_specs=pl.BlockSpec((tm, tn), lambda i, j: (i, j)),
        out_shape=jax.ShapeDtypeStruct((t, dff), BF16),
        scratch_shapes=[pltpu.VMEM((tm + 2 * HALO, d), BF16)],
        compiler_params=_params("parallel", "arbitrary"),
    )(h, h, h, w_up, w_up, conv_w, conv_w, cb, cb)


def kernel(x, c, w_ada, b_ada, ada_table, norm1_g, w_in, hg_lb_logits, hg_norm_g, da_lambda, da_norm_g,
           w_out, norm2_g, w_up, conv_w, conv_b, w_down, final_g):
    bsz, t, d = x.shape
    assert bsz == 1
    depth = w_in.shape[0]
    d_hgrn = hg_lb_logits.shape[-1]
    d_diff = w_out.shape[1] - d_hgrn
    assert w_in.shape[2] == 5 * d_hgrn + 3 * d_diff

    mod = _ada_mod(c, w_ada, b_ada, ada_table)
    wb_in, wb_out, wb_up, wb_down = (w[0].astype(BF16) for w in (w_in, w_out, w_up, w_down))
    xs = x.reshape(t, d)
    for l in range(depth):
        more = l + 1 < depth

        def cast(*ws):
            return [(w, l + 1) for w in ws] if more else []

        m = [mod[l, i * d:(i + 1) * d] for i in range(N_MOD)]
        h = _rmsnorm(xs, norm1_g[l], m[0], m[1])
        proj, nxt_in_up = _matmul([h], wb_in, tm=1024, tn=1024, cast=cast(w_in, w_up))
        o_fwd = _hgrn_pass(proj, hg_lb_logits, l, d_hgrn, reverse=False)
        o_hg = _hgrn_pass(proj, hg_lb_logits, l, d_hgrn, reverse=True, o_fwd=o_fwd, norm_g=hg_norm_g[l])
        o_da = _diff_attention(proj, da_lambda[l], da_norm_g[l], l, d_hgrn, d_diff)
        if d_hgrn == d_diff:
            mix = [o_hg, o_da]
        else:
            mix = [jnp.concatenate([o_hg, o_da], axis=1)]
        xs, nxt_out = _matmul(mix, wb_out, tm=1024, tn=512, x=xs, gate=m[2], cast=cast(w_out))
        h = _rmsnorm(xs, norm2_g[l], m[3], m[4])
        act = _up_conv_glu(h, wb_up, conv_w[l], conv_b[l])
        xs, nxt_down = _matmul([act], wb_down, tm=512, tn=512, x=xs, gate=m[5], cast=cast(w_down))
        if more:
            (wb_in, wb_up), (wb_out,), (wb_down,) = nxt_in_up, nxt_out, nxt_down
    out = _rmsnorm(xs, final_g, out_dtype=F32)
    return out.reshape(bsz, t, d)
```

```python
import functools
import math

import jax
import jax.numpy as jnp
import numpy as np
from jax import lax
from jax.experimental import pallas as pl
from jax.experimental.pallas import tpu as pltpu

F32 = jnp.float32
BF16 = jnp.bfloat16

HG_HEAD_DIM = 128
DA_HEAD_DIM = 128
DA_V_DIM = 2 * DA_HEAD_DIM
N_MOD = 6
CONV_WIDTH = 3
HALO = 16
EPS = 1e-6
HG_LEAF = 16
HG_LEAF_CLAMP = 80.0
DA_SKIP_MARGIN = 104.5
DA_FAST_RISE = 60.0
DA_FAST_GROUPS = (6, 2, 1)
VMEM_LIMIT_BYTES = 56 * 1024 * 1024


def _params(*semantics):
    return pltpu.CompilerParams(dimension_semantics=semantics, vmem_limit_bytes=VMEM_LIMIT_BYTES)


def _dot(a, b):
    return lax.dot_general(a, b, (((1,), (0,)), ((), ())), preferred_element_type=F32)


def _dot_nt(a, b):
    return lax.dot_general(a, b, (((1,), (1,)), ((), ())), preferred_element_type=F32)


def _dot_tn(a, b):
    return lax.dot_general(a, b, (((0,), (0,)), ((), ())), preferred_element_type=F32)


def _sigmoid(x):
    return 1.0 / (1.0 + jnp.exp(-x))


def _pick(n, *cands):
    for c in cands:
        if n % c == 0:
            return c
    return n


def _mod_kernel(c_ref, w_ref, b_ref, t_ref, o_ref):
    c = c_ref[...]
    s = c * _sigmoid(c)
    y = jnp.dot(s, w_ref[...], preferred_element_type=F32, precision=lax.Precision.HIGHEST)
    o_ref[...] = y[0:1, :] + b_ref[...] + t_ref[...]


def _ada_mod(c, w_ada, b_ada, ada_table):
    _, d = c.shape
    depth = ada_table.shape[0]
    n = w_ada.shape[1]
    tn = _pick(n, 512, 256, 128)
    c8 = jnp.broadcast_to(c, (8, d))
    return pl.pallas_call(
        _mod_kernel,
        grid=(n // tn,),
        in_specs=[pl.BlockSpec((8, d), lambda j: (0, 0)),
                  pl.BlockSpec((d, tn), lambda j: (0, j)),
                  pl.BlockSpec((1, tn), lambda j: (0, j)),
                  pl.BlockSpec((depth, tn), lambda j: (0, j))],
        out_specs=pl.BlockSpec((depth, tn), lambda j: (0, j)),
        out_shape=jax.ShapeDtypeStruct((depth, n), F32),
        compiler_params=_params("parallel"),
    )(c8, w_ada, b_ada.reshape(1, n), ada_table.reshape(depth, n))


def _norm_kernel(x_ref, g_ref, *rest, modulated):
    if modulated:
        shift_ref, scale_ref, o_ref = rest
    else:
        (o_ref,) = rest
    x = x_ref[...]
    y = x * lax.rsqrt(jnp.mean(x * x, axis=-1, keepdims=True) + EPS) * g_ref[...]
    if modulated:
        y = y * (1.0 + scale_ref[...]) + shift_ref[...]
    o_ref[...] = y.astype(o_ref.dtype)


def _rmsnorm(x, g, shift=None, scale=None, out_dtype=BF16):
    t, d = x.shape
    tr = _pick(t, 256, 128, 64, 32, 16, 8)
    modulated = shift is not None
    row = pl.BlockSpec((1, d), lambda i: (0, 0))
    args = [x, g.reshape(1, d)] + ([shift.reshape(1, d), scale.reshape(1, d)] if modulated else [])
    return pl.pallas_call(
        functools.partial(_norm_kernel, modulated=modulated),
        grid=(t // tr,),
        in_specs=[pl.BlockSpec((tr, d), lambda i: (i, 0))] + [row] * (len(args) - 1),
        out_specs=pl.BlockSpec((tr, d), lambda i: (i, 0)),
        out_shape=jax.ShapeDtypeStruct((t, d), out_dtype),
        compiler_params=_params("parallel"),
    )(*args)


def _cast_job(w_all, layer, grid):
    _, k, n = w_all.shape
    gi, gj = grid
    assert k % gi == 0 and n % gj == 0
    bk, bn = k // gi, n // gj
    assert bk % 16 == 0 and bn % 128 == 0
    return (pl.BlockSpec((None, bk, bn), lambda i, j: (layer, i, j)),
            pl.BlockSpec((bk, bn), lambda i, j: (i, j)),
            jax.ShapeDtypeStruct((k, n), BF16))


def _matmul_kernel(*refs, n_a, residual, n_cast):
    a_refs = refs[:n_a]
    w_ref = refs[n_a]
    n_in = n_a + 1 + (2 if residual else 0) + n_cast
    o_ref = refs[n_in]
    ka = a_refs[0].shape[1]
    acc = _dot(a_refs[0][...], w_ref[0:ka, :])
    for p in range(1, n_a):
        acc = acc + _dot(a_refs[p][...], w_ref[p * ka:(p + 1) * ka, :])
    if residual:
        x_ref, gate_ref = refs[n_a + 1], refs[n_a + 2]
        o_ref[...] = x_ref[...] + gate_ref[...] * acc
    else:
        o_ref[...] = acc.astype(o_ref.dtype)
    for c in range(n_cast):
        refs[n_in + 1 + c][...] = refs[n_in - n_cast + c][...].astype(BF16)


def _matmul(a_list, w, *, tm, tn, out_dtype=BF16, x=None, gate=None, cast=()):
    m, ka = a_list[0].shape
    n_a = len(a_list)
    kt, n = w.shape
    assert kt == n_a * ka and all(a.shape == (m, ka) for a in a_list)
    tm, tn = min(tm, m), min(tn, n)
    assert m % tm == 0 and n % tn == 0
    grid = (m // tm, n // tn)
    residual = x is not None
    in_specs = [pl.BlockSpec((tm, ka), lambda i, j: (i, 0)) for _ in range(n_a)]
    in_specs.append(pl.BlockSpec((kt, tn), lambda i, j: (0, j)))
    args = list(a_list) + [w]
    if residual:
        in_specs += [pl.BlockSpec((tm, tn), lambda i, j: (i, j)),
                     pl.BlockSpec((1, tn), lambda i, j: (0, j))]
        args += [x, gate.reshape(1, n)]
        out_dtype = F32
    out_specs = [pl.BlockSpec((tm, tn), lambda i, j: (i, j))]
    out_shape = [jax.ShapeDtypeStruct((m, n), out_dtype)]
    for w_all, layer in cast:
        c_in, c_out, c_shape = _cast_job(w_all, layer, grid)
        in_specs.append(c_in)
        args.append(w_all)
        out_specs.append(c_out)
        out_shape.append(c_shape)
    res = pl.pallas_call(
        functools.partial(_matmul_kernel, n_a=n_a, residual=residual, n_cast=len(cast)),
        grid=grid,
        in_specs=in_specs,
        out_specs=out_specs,
        out_shape=out_shape,
        compiler_params=_params("parallel", "parallel"),
    )(*args)
    return res[0], list(res[1:])


def _hgrn_level_ids(r, reverse):
    t = np.arange(r)[:, None]
    s = np.arange(r)[None, :]
    ids = np.full((r, r), -1, np.int32)
    tri = (s >= t) if reverse else (s <= t)
    ids[(t // HG_LEAF == s // HG_LEAF) & tri] = 0
    lv, i = HG_LEAF, 1
    while 2 * lv <= r:
        same = (t // (2 * lv)) == (s // (2 * lv))
        t_hi, s_hi = (t & lv) != 0, (s & lv) != 0
        pair = (~t_hi & s_hi) if reverse else (t_hi & ~s_hi)
        ids[same & pair] = i
        lv, i = 2 * lv, i + 1
    return ids


def _hgrn_kernel(q_ref, z_ref, v_ref, lbl_ref, ids_ref, *rest, reverse, layer, heads, final):
    if final:
        ofwd_ref, gate_ref, ng_ref, o_ref, st_ref = rest
    else:
        o_ref, st_ref = rest
    r = q_ref.shape[0]
    hd = HG_HEAD_DIM

    @pl.when(pl.program_id(1) == 0)
    def _():
        st_ref[...] = jnp.zeros_like(st_ref)

    logits = lbl_ref[0]
    e = jnp.exp(logits - jnp.max(logits, axis=0, keepdims=True))
    lb_all = (jnp.sum(e[1:layer + 1], axis=0, keepdims=True) / jnp.sum(e, axis=0, keepdims=True)
              if layer > 0 else jnp.zeros_like(e[0:1]))

    ids = ids_ref[...]
    n_levels = (r // HG_LEAF).bit_length()
    masks = [ids == level for level in range(n_levels)]
    row = lax.broadcasted_iota(jnp.int32, (r, r), 0)
    col = lax.broadcasted_iota(jnp.int32, (r, r), 1)
    tri = jnp.where((col >= row) if reverse else (col <= row), 1.0, 0.0).astype(BF16)
    total_row = 0 if reverse else r - 1

    for h in range(heads):
        sl = slice(h * hd, (h + 1) * hd)
        lb = lb_all[:, sl]
        q = q_ref[:, sl].astype(F32) * (hd ** -0.5)
        z = z_ref[:, sl].astype(F32)
        v = v_ref[:, sl]
        sig = _sigmoid(z)
        g = jnp.log(lb + (1.0 - lb) * sig)
        kc = (1.0 - lb) * (1.0 - sig)

        g1 = g.astype(BF16)
        rem = g - g1.astype(F32)
        g2 = rem.astype(BF16)
        g3 = (rem - g2.astype(F32)).astype(BF16)
        b = _dot(tri, g1) + _dot(tri, g2) + _dot(tri, g3)

        a = jnp.zeros((r, r), F32)
        lv, level = HG_LEAF // 2, 0
        while 2 * lv <= r:
            blk = 2 * lv
            nb = r // blk
            ref_row = lv if reverse else lv - 1
            bb = b.reshape(nb, blk, hd)
            d = bb - bb[:, ref_row:ref_row + 1, :]
            if level == 0:
                eq = jnp.exp(jnp.minimum(d, HG_LEAF_CLAMP))
                ek = jnp.exp(jnp.minimum(-d, HG_LEAF_CLAMP))
            else:
                eq = ek = jnp.exp(-jnp.abs(d))
            qe = (q.reshape(nb, blk, hd) * eq).reshape(r, hd).astype(BF16)
            ke = (kc.reshape(nb, blk, hd) * ek).reshape(r, hd).astype(BF16)
            a = jnp.where(masks[level], _dot_nt(qe, ke), a)
            lv, level = 2 * lv, level + 1
        assert level == n_levels

        st = st_ref[h]
        o = _dot(a.astype(BF16), v) + _dot_nt((q * jnp.exp(b)).astype(BF16), st.astype(BF16))
        bl = b[total_row:total_row + 1, :]
        kd = (kc * jnp.exp(bl - b)).astype(BF16)
        st_ref[h] = jnp.exp(bl) * st + _dot_tn(v, kd)

        if final:
            o = o + ofwd_ref[:, sl]
            y = o * lax.rsqrt(jnp.mean(o * o, axis=-1, keepdims=True) + EPS) * ng_ref[...]
            gt = gate_ref[:, sl].astype(F32)
            o_ref[:, sl] = (y * (gt * _sigmoid(gt))).astype(o_ref.dtype)
        else:
            o_ref[:, sl] = o


def _hgrn_pass(proj, lb_logits, layer, d_hgrn, *, reverse, o_fwd=None, norm_g=None, chunk=256):
    t = proj.shape[0]
    heads_total = d_hgrn // HG_HEAD_DIM
    hb = _pick(heads_total, 16, 8, 4, 2, 1)
    w = hb * HG_HEAD_DIM
    r = min(chunk, t)
    assert t % r == 0 and r % (2 * HG_LEAF) == 0
    nchunks = t // r
    cpb = d_hgrn // w
    final = o_fwd is not None
    depth = lb_logits.shape[1]

    def rows(n):
        return (nchunks - 1 - n) if reverse else n

    def col_spec(group):
        return pl.BlockSpec((r, w), lambda h, n: (rows(n), group * cpb + h))

    in_specs = [col_spec(0), col_spec(2 if reverse else 1), col_spec(3),
                pl.BlockSpec((1, depth, w), lambda h, n: (1 if reverse else 0, 0, h)),
                pl.BlockSpec((r, r), lambda h, n: (0, 0))]
    args = [proj, proj, proj, lb_logits, jnp.asarray(_hgrn_level_ids(r, reverse))]
    if final:
        in_specs += [pl.BlockSpec((r, w), lambda h, n: (rows(n), h)), col_spec(4),
                     pl.BlockSpec((1, HG_HEAD_DIM), lambda h, n: (0, 0))]
        args += [o_fwd, proj, norm_g.reshape(1, HG_HEAD_DIM)]
    return pl.pallas_call(
        functools.partial(_hgrn_kernel, reverse=reverse, layer=layer, heads=hb, final=final),
        grid=(heads_total // hb, nchunks),
        in_specs=in_specs,
        out_specs=pl.BlockSpec((r, w), lambda h, n: (rows(n), h)),
        out_shape=jax.ShapeDtypeStruct((t, d_hgrn), BF16 if final else F32),
        scratch_shapes=[pltpu.VMEM((hb, HG_HEAD_DIM, HG_HEAD_DIM), F32)],
        compiler_params=_params("parallel", "arbitrary"),
    )(*args)


def _scaled_q(q_ref):
    return (q_ref[...].astype(F32) * (DA_HEAD_DIM ** -0.5)).astype(BF16)


def _max_sq_norm(x):
    sq = x.astype(F32)
    sq = sq * sq
    n = jnp.maximum(jnp.sum(sq[:, :DA_HEAD_DIM], axis=-1, keepdims=True),
                    jnp.sum(sq[:, DA_HEAD_DIM:], axis=-1, keepdims=True))
    return jnp.max(n, axis=0, keepdims=True)


def _split3(x):
    hi = x.astype(BF16).astype(F32)
    rem = x - hi
    mid = rem.astype(BF16).astype(F32)
    lo = (rem - mid).astype(BF16).astype(F32)
    return (hi, mid, lo)


def _lane_fields(shape, fields, first=0):
    lane = lax.broadcasted_iota(jnp.int32, shape, 1)
    out = jnp.zeros(shape, F32)
    for k, val in enumerate(fields):
        out = jnp.where(lane == first + k, val, out)
    return out


def _lane_block_sum(p):
    out = p[:, 0:128]
    for kb in range(1, p.shape[1] // 128):
        out = out + p[:, kb * 128:(kb + 1) * 128]
    return out


def _diff_attn_kernel(q_ref, k_ref, v_ref, slope_ref, lam_ref, ng_ref, o_ref,
                      qs_ref, bias_ref, kmax_ref, m_ref, l_ref, acc_ref, qx_ref, kx_ref, qxs_ref,
                      *, lam_init, tk, chunk):
    i = pl.program_id(1)
    tq = q_ref.shape[0]
    nkv = k_ref.shape[0] // tk
    hd = DA_HEAD_DIM
    c = slope_ref[0][0:1, 0:1]

    @pl.when(i == 0)
    def _():
        a = lax.broadcasted_iota(jnp.int32, (tq, tk), 0)
        b = lax.broadcasted_iota(jnp.int32, (tq, tk), 1)
        d = (a - b).astype(F32) * c
        bias_ref[0] = -d
        bias_ref[1] = d
        bias_ref[2] = -jnp.abs(d)
        rowk = lax.broadcasted_iota(jnp.int32, (tk, 1), 0).astype(F32) * c
        kx_ref[...] = _lane_fields((tk, 128), (1.0,) * 3 + _split3(rowk) + (1.0,)).astype(BF16)
        rowq = lax.broadcasted_iota(jnp.int32, (tq, 1), 0).astype(F32) * c
        for side_idx, sgn in enumerate((1.0, -1.0)):
            qxs_ref[side_idx] = _lane_fields((tq, 128), _split3(-sgn * rowq) + (sgn,) * 3)

        def body(r, acc):
            rows = k_ref[pl.ds(pl.multiple_of(r * chunk, chunk), chunk), :]
            return jnp.maximum(acc, _max_sq_norm(rows))
        kmax = lax.fori_loop(0, k_ref.shape[0] // chunk, body, jnp.zeros((1, 1), F32))
        kmax_ref[...] = jnp.broadcast_to(kmax, kmax_ref.shape)

    qs = _scaled_q(q_ref)
    qs_ref[...] = qs
    bound = jnp.sqrt(_max_sq_norm(qs) * kmax_ref[0:1, 0:1])

    def tile(j):
        off = pl.multiple_of(j * tk, tk)
        cij = -c * jnp.abs(i * tq - j * tk).astype(F32)
        return k_ref[pl.ds(off, tk), :], v_ref[pl.ds(off, tk), :], cij

    def scores(mp, kt, bias):
        sl = slice(mp * hd, (mp + 1) * hd)
        return _dot_nt(qs_ref[:, sl], kt[:, sl]) + bias

    kt, vt, _ = tile(i)
    ps, m_min = [], None
    for mp in range(2):
        s = scores(mp, kt, bias_ref[2])
        m = jnp.max(s, axis=-1, keepdims=True).astype(BF16).astype(F32)
        p = jnp.exp(s - m)
        m_ref[mp] = m
        l_ref[mp] = _lane_block_sum(p)
        ps.append(p.astype(BF16))
        m_low = jnp.min(m, axis=0, keepdims=True)
        m_min = m_low if m_min is None else jnp.minimum(m_min, m_low)
    acc_ref[...] = _dot(jnp.concatenate(ps, axis=0), vt)

    rise = bound - m_min
    w = jnp.floor((rise + DA_SKIP_MARGIN) / (c * tk)) + 1.0
    w = jnp.minimum(w, float(nkv)).astype(jnp.int32)[0, 0]
    fast = jnp.where(rise < DA_FAST_RISE, 1, 0)[0, 0]
    lo = jnp.maximum(i - w, 0)
    n_other = jnp.minimum(i + w, nkv - 1) - lo

    def other_tile(t):
        j = lo + t
        j = j + jnp.where(j >= i, 1, 0)
        side = jnp.where(j < i, 0, 1)
        return tile(j) + (side,)

    for mp in range(2):
        m_cols = _lane_fields((tq, 128), (-m_ref[mp],), first=6)
        for side_idx in range(2):
            qx_ref[2 * side_idx + mp] = (qxs_ref[side_idx] + m_cols).astype(BF16)

    def fast_tiles(ts):
        probs, vals = [], []
        lsum = [None, None]
        for t in ts:
            kt, vt, cij, side = other_tile(t)
            ps = []
            for mp in range(2):
                sl = slice(mp * hd, (mp + 1) * hd)
                qcat = jnp.concatenate([qs_ref[:, sl], qx_ref[2 * side + mp]], axis=1)
                kcat = jnp.concatenate([kt[:, sl], kx_ref[...]], axis=1)
                p = jnp.exp(_dot_nt(qcat, kcat) + cij)
                part = _lane_block_sum(p)
                lsum[mp] = part if lsum[mp] is None else lsum[mp] + part
                ps.append(p.astype(BF16))
            probs.append(jnp.concatenate(ps, axis=0))
            vals.append(vt)
        for mp in range(2):
            l_ref[mp] += lsum[mp]
        acc_ref[...] += _dot(jnp.concatenate(probs, axis=1), jnp.concatenate(vals, axis=0))


    def safe_body(t, carry):
        kt, vt, cij, side = other_tile(t)
        bias = bias_ref[side]
        ps, alphas = [], []
        for mp in range(2):
            s = scores(mp, kt, bias)
            m_old = m_ref[mp]
            m_new = jnp.maximum(m_old, jnp.max(s, axis=-1, keepdims=True) + cij)
            alpha = jnp.exp(m_old - m_new)
            p = jnp.exp(s - (m_new - cij))
            l_ref[mp] = alpha * l_ref[mp] + _lane_block_sum(p)
            m_ref[mp] = m_new
            ps.append(p.astype(BF16))
            alphas.append(alpha)
        pv = _dot(jnp.concatenate(ps, axis=0), vt)
        for mp in range(2):
            rows = slice(mp * tq, (mp + 1) * tq)
            acc_ref[rows, :] = alphas[mp] * acc_ref[rows, :] + pv[rows]
        return carry

    n_fast = jnp.where(fast > 0, n_other, 0)
    start = 0
    for group in DA_FAST_GROUPS:
        count = (n_fast - start) // group

        def fast_group(t, carry, start=start, group=group):
            fast_tiles(tuple(start + group * t + g for g in range(group)))
            return carry

        lax.fori_loop(0, count, fast_group, 0)
        start = start + count * group
    lax.fori_loop(0, n_other - n_fast, safe_body, 0)

    lp = lam_ref[...]
    lam = (jnp.exp(jnp.sum(lp[0:1] * lp[1:2], axis=-1, keepdims=True))
           - jnp.exp(jnp.sum(lp[2:3] * lp[3:4], axis=-1, keepdims=True)) + lam_init)
    l0 = jnp.sum(l_ref[0], axis=-1, keepdims=True)
    l1 = jnp.sum(l_ref[1], axis=-1, keepdims=True)
    o = acc_ref[0:tq, :] / l0 - lam * (acc_ref[tq:2 * tq, :] / l1)
    y = o * lax.rsqrt(jnp.mean(o * o, axis=-1, keepdims=True) + EPS) * ng_ref[...]
    o_ref[...] = (y * (1.0 - lam_init)).astype(o_ref.dtype)


def _diff_attention(proj, lam_params, norm_g, layer, d_hgrn, d_diff, *, tile=512):
    t = proj.shape[0]
    heads = d_diff // DA_V_DIM
    tq = tk = min(tile, t)
    assert t % tq == 0 and tk % 128 == 0
    nq = t // tq
    lam_init = 0.8 - 0.6 * math.exp(-0.3 * layer)
    base = 5 * d_hgrn // DA_V_DIM
    slopes = np.exp2(-8.0 * np.arange(1, heads + 1, dtype=np.float32) / heads).astype(np.float32)
    slopes = jnp.asarray(np.broadcast_to(slopes[:, None, None], (heads, 1, 128)))

    return pl.pallas_call(
        functools.partial(_diff_attn_kernel, lam_init=lam_init, tk=tk, chunk=min(1024, t)),
        grid=(heads, nq),
        in_specs=[pl.BlockSpec((tq, DA_V_DIM), lambda h, i: (i, base + h)),
                  pl.BlockSpec((t, DA_V_DIM), lambda h, i: (0, base + heads + h)),
                  pl.BlockSpec((t, DA_V_DIM), lambda h, i: (0, base + 2 * heads + h)),
                  pl.BlockSpec((1, 1, 128), lambda h, i: (h, 0, 0)),
                  pl.BlockSpec((4, DA_HEAD_DIM), lambda h, i: (0, 0)),
                  pl.BlockSpec((1, DA_V_DIM), lambda h, i: (0, 0))],
        out_specs=pl.BlockSpec((tq, DA_V_DIM), lambda h, i: (i, h)),
        out_shape=jax.ShapeDtypeStruct((t, d_diff), BF16),
        scratch_shapes=[pltpu.VMEM((tq, DA_V_DIM), BF16),
                        pltpu.VMEM((3, tq, tk), F32),
                        pltpu.VMEM((8, 128), F32),
                        pltpu.VMEM((2, tq, 1), F32),
                        pltpu.VMEM((2, tq, 128), F32),
                        pltpu.VMEM((2 * tq, DA_V_DIM), F32),
                        pltpu.VMEM((4, tq, 128), BF16),
                        pltpu.VMEM((tk, 128), BF16),
                        pltpu.VMEM((2, tq, 128), F32)],
        compiler_params=_params("parallel", "arbitrary"),
    )(proj, proj, proj, slopes, lam_params, norm_g.reshape(1, DA_V_DIM))


def _up_conv_glu_kernel(h_ref, hp_ref, hn_ref, wg_ref, wv_ref, cwg_ref, cwv_ref, cbg_ref, cbv_ref,
                        o_ref, hext_ref):
    i, j = pl.program_id(0), pl.program_id(1)
    tm = h_ref.shape[0]
    ext = hext_ref.shape[0]

    @pl.when(j == 0)
    def _():
        hext_ref[HALO:HALO + tm, :] = h_ref[...]

        @pl.when(i > 0)
        def _():
            hext_ref[0:HALO, :] = hp_ref[...]

        @pl.when(i == 0)
        def _():
            hext_ref[0:HALO, :] = jnp.zeros_like(hp_ref)

        @pl.when(i < pl.num_programs(0) - 1)
        def _():
            hext_ref[HALO + tm:ext, :] = hn_ref[...]

        @pl.when(i == pl.num_programs(0) - 1)
        def _():
            hext_ref[HALO + tm:ext, :] = jnp.zeros_like(hn_ref)

    hext = hext_ref[...]

    def conv(w_ref, cw_ref, cb_ref):
        u = _dot(hext, w_ref[...])
        up = pltpu.roll(u, 1, axis=0)[HALO:HALO + tm]
        un = pltpu.roll(u, ext - 1, axis=0)[HALO:HALO + tm]
        cw = cw_ref[...]
        return up * cw[0:1] + u[HALO:HALO + tm] * cw[1:2] + un * cw[2:3] + cb_ref[...]

    gate = conv(wg_ref, cwg_ref, cbg_ref)
    val = conv(wv_ref, cwv_ref, cbv_ref)
    o_ref[...] = (gate * _sigmoid(gate) * val).astype(o_ref.dtype)


def _up_conv_glu(h, w_up, conv_w, conv_b, *, tm=1024, tn=512):
    t, d = h.shape
    n2 = w_up.shape[1]
    dff = n2 // 2
    tm, tn = min(tm, t), min(tn, dff)
    assert t % tm == 0 and dff % tn == 0 and tm % HALO == 0
    ncb = dff // tn
    rh = tm // HALO
    last = t // HALO - 1
    grid = (t // tm, ncb)

    def wcol(off):
        return pl.BlockSpec((d, tn), lambda i, j: (0, off + j))

    def prm(rows, off):
        return pl.BlockSpec((rows, tn), lambda i, j: (0, off + j))

    cb = conv_b.reshape(1, n2)
    return pl.pallas_call(
        _up_conv_glu_kernel,
        grid=grid,
        in_specs=[pl.BlockSpec((tm, d), lambda i, j: (i, 0)),
                  pl.BlockSpec((HALO, d), lambda i, j: (jnp.maximum(i * rh - 1, 0), 0)),
                  pl.BlockSpec((HALO, d), lambda i, j: (jnp.minimum((i + 1) * rh, last), 0)),
                  wcol(0), wcol(ncb),
                  prm(CONV_WIDTH, 0), prm(CONV_WIDTH, ncb), prm(1, 0), prm(1, ncb)],
        out_specs=pl.BlockSpec((tm, tn), lambda i, j: (i, j)),
        out_shape=jax.ShapeDtypeStruct((t, dff), BF16),
        scratch_shapes=[pltpu.VMEM((tm + 2 * HALO, d), BF16)],
        compiler_params=_params("parallel", "arbitrary"),
    )(h, h, h, w_up, w_up, conv_w, conv_w, cb, cb)


def kernel(x, c, w_ada, b_ada, ada_table, norm1_g, w_in, hg_lb_logits, hg_norm_g, da_lambda, da_norm_g,
           w_out, norm2_g, w_up, conv_w, conv_b, w_down, final_g):
    bsz, t, d = x.shape
    assert bsz == 1
    depth = w_in.shape[0]
    d_hgrn = hg_lb_logits.shape[-1]
    d_diff = w_out.shape[1] - d_hgrn
    assert w_in.shape[2] == 5 * d_hgrn + 3 * d_diff

    mod = _ada_mod(c, w_ada, b_ada, ada_table)
    wb_in, wb_out, wb_up, wb_down = (w[0].astype(BF16) for w in (w_in, w_out, w_up, w_down))
    xs = x.reshape(t, d)
    for l in range(depth):
        more = l + 1 < depth

        def cast(*ws):
            return [(w, l + 1) for w in ws] if more else []

        m = [mod[l, i * d:(i + 1) * d] for i in range(N_MOD)]
        h = _rmsnorm(xs, norm1_g[l], m[0], m[1])
        proj, nxt_in_up = _matmul([h], wb_in, tm=1024, tn=1024, cast=cast(w_in, w_up))
        o_fwd = _hgrn_pass(proj, hg_lb_logits, l, d_hgrn, reverse=False)
        o_hg = _hgrn_pass(proj, hg_lb_logits, l, d_hgrn, reverse=True, o_fwd=o_fwd, norm_g=hg_norm_g[l])
        o_da = _diff_attention(proj, da_lambda[l], da_norm_g[l], l, d_hgrn, d_diff)
        if d_hgrn == d_diff:
            mix = [o_hg, o_da]
        else:
            mix = [jnp.concatenate([o_hg, o_da], axis=1)]
        xs, nxt_out = _matmul(mix, wb_out, tm=1024, tn=512, x=xs, gate=m[2], cast=cast(w_out))
        h = _rmsnorm(xs, norm2_g[l], m[3], m[4])
        act = _up_conv_glu(h, wb_up, conv_w[l], conv_b[l])
        xs, nxt_down = _matmul([act], wb_down, tm=1024, tn=256, x=xs, gate=m[5], cast=cast(w_down))
        if more:
            (wb_in, wb_up), (wb_out,), (wb_down,) = nxt_in_up, nxt_out, nxt_down
    out = _rmsnorm(xs, final_g, out_dtype=F32)
    return out.reshape(bsz, t, d)
```

```python
import functools
import math

import jax
import jax.numpy as jnp
import numpy as np
from jax import lax
from jax.experimental import pallas as pl
from jax.experimental.pallas import tpu as pltpu

F32 = jnp.float32
BF16 = jnp.bfloat16

HG_HEAD_DIM = 128
DA_HEAD_DIM = 128
DA_V_DIM = 2 * DA_HEAD_DIM
N_MOD = 6
CONV_WIDTH = 3
NORM_CHUNK = 512
HALO = 16
EPS = 1e-6
HG_LEAF = 16
HG_LEAF_CLAMP = 80.0
DA_SKIP_MARGIN = 104.5
DA_FAST_RISE = 60.0
DA_FAST_GROUPS = (6, 2, 1)
VMEM_LIMIT_BYTES = 56 * 1024 * 1024


def _params(*semantics):
    return pltpu.CompilerParams(dimension_semantics=semantics, vmem_limit_bytes=VMEM_LIMIT_BYTES)


def _dot(a, b):
    return lax.dot_general(a, b, (((1,), (0,)), ((), ())), preferred_element_type=F32)


def _dot_nt(a, b):
    return lax.dot_general(a, b, (((1,), (1,)), ((), ())), preferred_element_type=F32)


def _dot_tn(a, b):
    return lax.dot_general(a, b, (((0,), (0,)), ((), ())), preferred_element_type=F32)


def _sigmoid(x):
    return 1.0 / (1.0 + jnp.exp(-x))


def _pick(n, *cands):
    for c in cands:
        if n % c == 0:
            return c
    return n


def _mod_kernel(c_ref, w_ref, b_ref, t_ref, o_ref):
    c = c_ref[...]
    s = c * _sigmoid(c)
    y = jnp.dot(s, w_ref[...], preferred_element_type=F32, precision=lax.Precision.HIGHEST)
    o_ref[...] = y[0:1, :] + b_ref[...] + t_ref[...]


def _ada_mod(c, w_ada, b_ada, ada_table):
    _, d = c.shape
    depth = ada_table.shape[0]
    n = w_ada.shape[1]
    tn = _pick(n, 512, 256, 128)
    c8 = jnp.broadcast_to(c, (8, d))
    return pl.pallas_call(
        _mod_kernel,
        grid=(n // tn,),
        in_specs=[pl.BlockSpec((8, d), lambda j: (0, 0)),
                  pl.BlockSpec((d, tn), lambda j: (0, j)),
                  pl.BlockSpec((1, tn), lambda j: (0, j)),
                  pl.BlockSpec((depth, tn), lambda j: (0, j))],
        out_specs=pl.BlockSpec((depth, tn), lambda j: (0, j)),
        out_shape=jax.ShapeDtypeStruct((depth, n), F32),
        compiler_params=_params("parallel"),
    )(c8, w_ada, b_ada.reshape(1, n), ada_table.reshape(depth, n))


def _norm_kernel(x_ref, g_ref, *rest, modulated):
    if modulated:
        shift_ref, scale_ref, o_ref = rest
    else:
        (o_ref,) = rest
    d = x_ref.shape[1]
    chunk = min(NORM_CHUNK, d)
    part = None
    for c0 in range(0, d, chunk):
        xc = x_ref[:, c0:c0 + chunk]
        sq = _lane_block_sum(xc * xc)
        part = sq if part is None else part + sq
    inv = lax.rsqrt(jnp.sum(part, axis=-1, keepdims=True) * (1.0 / d) + EPS)
    for c0 in range(0, d, chunk):
        cols = slice(c0, c0 + chunk)
        gain = g_ref[:, cols]
        if modulated:
            gain = gain * (1.0 + scale_ref[:, cols])
        y = x_ref[:, cols] * inv * gain
        if modulated:
            y = y + shift_ref[:, cols]
        o_ref[:, cols] = y.astype(o_ref.dtype)


def _rmsnorm(x, g, shift=None, scale=None, out_dtype=BF16):
    t, d = x.shape
    tr = _pick(t, 256, 128, 64, 32, 16, 8)
    modulated = shift is not None
    row = pl.BlockSpec((1, d), lambda i: (0, 0))
    args = [x, g.reshape(1, d)] + ([shift.reshape(1, d), scale.reshape(1, d)] if modulated else [])
    return pl.pallas_call(
        functools.partial(_norm_kernel, modulated=modulated),
        grid=(t // tr,),
        in_specs=[pl.BlockSpec((tr, d), lambda i: (i, 0))] + [row] * (len(args) - 1),
        out_specs=pl.BlockSpec((tr, d), lambda i: (i, 0)),
        out_shape=jax.ShapeDtypeStruct((t, d), out_dtype),
        compiler_params=_params("parallel"),
    )(*args)


def _cast_job(w_all, layer, grid):
    _, k, n = w_all.shape
    gi, gj = grid
    assert k % gi == 0 and n % gj == 0
    bk, bn = k // gi, n // gj
    assert bk % 16 == 0 and bn % 128 == 0
    return (pl.BlockSpec((None, bk, bn), lambda i, j: (layer, i, j)),
            pl.BlockSpec((bk, bn), lambda i, j: (i, j)),
            jax.ShapeDtypeStruct((k, n), BF16))


def _matmul_kernel(*refs, n_a, residual, n_cast):
    a_refs = refs[:n_a]
    w_ref = refs[n_a]
    n_in = n_a + 1 + (2 if residual else 0) + n_cast
    o_ref = refs[n_in]
    ka = a_refs[0].shape[1]
    acc = _dot(a_refs[0][...], w_ref[0:ka, :])
    for p in range(1, n_a):
        acc = acc + _dot(a_refs[p][...], w_ref[p * ka:(p + 1) * ka, :])
    if residual:
        x_ref, gate_ref = refs[n_a + 1], refs[n_a + 2]
        o_ref[...] = x_ref[...] + gate_ref[...] * acc
    else:
        o_ref[...] = acc.astype(o_ref.dtype)
    for c in range(n_cast):
        refs[n_in + 1 + c][...] = refs[n_in - n_cast + c][...].astype(BF16)


def _matmul(a_list, w, *, tm, tn, out_dtype=BF16, x=None, gate=None, cast=()):
    m, ka = a_list[0].shape
    n_a = len(a_list)
    kt, n = w.shape
    assert kt == n_a * ka and all(a.shape == (m, ka) for a in a_list)
    tm, tn = min(tm, m), min(tn, n)
    assert m % tm == 0 and n % tn == 0
    grid = (m // tm, n // tn)
    residual = x is not None
    in_specs = [pl.BlockSpec((tm, ka), lambda i, j: (i, 0)) for _ in range(n_a)]
    in_specs.append(pl.BlockSpec((kt, tn), lambda i, j: (0, j)))
    args = list(a_list) + [w]
    if residual:
        in_specs += [pl.BlockSpec((tm, tn), lambda i, j: (i, j)),
                     pl.BlockSpec((1, tn), lambda i, j: (0, j))]
        args += [x, gate.reshape(1, n)]
        out_dtype = F32
    out_specs = [pl.BlockSpec((tm, tn), lambda i, j: (i, j))]
    out_shape = [jax.ShapeDtypeStruct((m, n), out_dtype)]
    for w_all, layer in cast:
        c_in, c_out, c_shape = _cast_job(w_all, layer, grid)
        in_specs.append(c_in)
        args.append(w_all)
        out_specs.append(c_out)
        out_shape.append(c_shape)
    res = pl.pallas_call(
        functools.partial(_matmul_kernel, n_a=n_a, residual=residual, n_cast=len(cast)),
        grid=grid,
        in_specs=in_specs,
        out_specs=out_specs,
        out_shape=out_shape,
        compiler_params=_params("parallel", "parallel"),
    )(*args)
    return res[0], list(res[1:])


def _hgrn_level_ids(r, reverse):
    t = np.arange(r)[:, None]
    s = np.arange(r)[None, :]
    ids = np.full((r, r), -1, np.int32)
    tri = (s >= t) if reverse else (s <= t)
    ids[(t // HG_LEAF == s // HG_LEAF) & tri] = 0
    lv, i = HG_LEAF, 1
    while 2 * lv <= r:
        same = (t // (2 * lv)) == (s // (2 * lv))
        t_hi, s_hi = (t & lv) != 0, (s & lv) != 0
        pair = (~t_hi & s_hi) if reverse else (t_hi & ~s_hi)
        ids[same & pair] = i
        lv, i = 2 * lv, i + 1
    return ids


def _hgrn_kernel(q_ref, z_ref, v_ref, lbl_ref, ids_ref, *rest, reverse, layer, heads, final):
    if final:
        ofwd_ref, gate_ref, ng_ref, o_ref, st_ref = rest
    else:
        o_ref, st_ref = rest
    r = q_ref.shape[0]
    hd = HG_HEAD_DIM

    @pl.when(pl.program_id(1) == 0)
    def _():
        st_ref[...] = jnp.zeros_like(st_ref)

    logits = lbl_ref[0]
    e = jnp.exp(logits - jnp.max(logits, axis=0, keepdims=True))
    lb_all = (jnp.sum(e[1:layer + 1], axis=0, keepdims=True) / jnp.sum(e, axis=0, keepdims=True)
              if layer > 0 else jnp.zeros_like(e[0:1]))

    ids = ids_ref[...]
    n_levels = (r // HG_LEAF).bit_length()
    masks = [ids == level for level in range(n_levels)]
    row = lax.broadcasted_iota(jnp.int32, (r, r), 0)
    col = lax.broadcasted_iota(jnp.int32, (r, r), 1)
    tri = jnp.where((col >= row) if reverse else (col <= row), 1.0, 0.0).astype(BF16)
    total_row = 0 if reverse else r - 1

    for h in range(heads):
        sl = slice(h * hd, (h + 1) * hd)
        lb = lb_all[:, sl]
        q = q_ref[:, sl].astype(F32) * (hd ** -0.5)
        z = z_ref[:, sl].astype(F32)
        v = v_ref[:, sl]
        sig = _sigmoid(z)
        g = jnp.log(lb + (1.0 - lb) * sig)
        kc = (1.0 - lb) * (1.0 - sig)

        g1 = g.astype(BF16)
        rem = g - g1.astype(F32)
        g2 = rem.astype(BF16)
        g3 = (rem - g2.astype(F32)).astype(BF16)
        b = _dot(tri, g1) + _dot(tri, g2) + _dot(tri, g3)

        a = jnp.zeros((r, r), F32)
        lv, level = HG_LEAF // 2, 0
        while 2 * lv <= r:
            blk = 2 * lv
            nb = r // blk
            ref_row = lv if reverse else lv - 1
            bb = b.reshape(nb, blk, hd)
            d = bb - bb[:, ref_row:ref_row + 1, :]
            if level == 0:
                eq = jnp.exp(jnp.minimum(d, HG_LEAF_CLAMP))
                ek = jnp.exp(jnp.minimum(-d, HG_LEAF_CLAMP))
            else:
                eq = ek = jnp.exp(-jnp.abs(d))
            qe = (q.reshape(nb, blk, hd) * eq).reshape(r, hd).astype(BF16)
            ke = (kc.reshape(nb, blk, hd) * ek).reshape(r, hd).astype(BF16)
            a = jnp.where(masks[level], _dot_nt(qe, ke), a)
            lv, level = 2 * lv, level + 1
        assert level == n_levels

        st = st_ref[h]
        o = _dot(a.astype(BF16), v) + _dot_nt((q * jnp.exp(b)).astype(BF16), st.astype(BF16))
        bl = b[total_row:total_row + 1, :]
        kd = (kc * jnp.exp(bl - b)).astype(BF16)
        st_ref[h] = jnp.exp(bl) * st + _dot_tn(v, kd)

        if final:
            o = o + ofwd_ref[:, sl]
            y = o * lax.rsqrt(jnp.mean(o * o, axis=-1, keepdims=True) + EPS) * ng_ref[...]
            gt = gate_ref[:, sl].astype(F32)
            o_ref[:, sl] = (y * (gt * _sigmoid(gt))).astype(o_ref.dtype)
        else:
            o_ref[:, sl] = o


def _hgrn_pass(proj, lb_logits, layer, d_hgrn, *, reverse, o_fwd=None, norm_g=None, chunk=256):
    t = proj.shape[0]
    heads_total = d_hgrn // HG_HEAD_DIM
    hb = _pick(heads_total, 16, 8, 4, 2, 1)
    w = hb * HG_HEAD_DIM
    r = min(chunk, t)
    assert t % r == 0 and r % (2 * HG_LEAF) == 0
    nchunks = t // r
    cpb = d_hgrn // w
    final = o_fwd is not None
    depth = lb_logits.shape[1]

    def rows(n):
        return (nchunks - 1 - n) if reverse else n

    def col_spec(group):
        return pl.BlockSpec((r, w), lambda h, n: (rows(n), group * cpb + h))

    in_specs = [col_spec(0), col_spec(2 if reverse else 1), col_spec(3),
                pl.BlockSpec((1, depth, w), lambda h, n: (1 if reverse else 0, 0, h)),
                pl.BlockSpec((r, r), lambda h, n: (0, 0))]
    args = [proj, proj, proj, lb_logits, jnp.asarray(_hgrn_level_ids(r, reverse))]
    if final:
        in_specs += [pl.BlockSpec((r, w), lambda h, n: (rows(n), h)), col_spec(4),
                     pl.BlockSpec((1, HG_HEAD_DIM), lambda h, n: (0, 0))]
        args += [o_fwd, proj, norm_g.reshape(1, HG_HEAD_DIM)]
    return pl.pallas_call(
        functools.partial(_hgrn_kernel, reverse=reverse, layer=layer, heads=hb, final=final),
        grid=(heads_total // hb, nchunks),
        in_specs=in_specs,
        out_specs=pl.BlockSpec((r, w), lambda h, n: (rows(n), h)),
        out_shape=jax.ShapeDtypeStruct((t, d_hgrn), BF16 if final else F32),
        scratch_shapes=[pltpu.VMEM((hb, HG_HEAD_DIM, HG_HEAD_DIM), F32)],
        compiler_params=_params("parallel", "arbitrary"),
    )(*args)


def _scaled_q(q_ref):
    return (q_ref[...].astype(F32) * (DA_HEAD_DIM ** -0.5)).astype(BF16)


def _max_sq_norm(x):
    sq = x.astype(F32)
    sq = sq * sq
    n = jnp.maximum(jnp.sum(sq[:, :DA_HEAD_DIM], axis=-1, keepdims=True),
                    jnp.sum(sq[:, DA_HEAD_DIM:], axis=-1, keepdims=True))
    return jnp.max(n, axis=0, keepdims=True)


def _split3(x):
    hi = x.astype(BF16).astype(F32)
    rem = x - hi
    mid = rem.astype(BF16).astype(F32)
    lo = (rem - mid).astype(BF16).astype(F32)
    return (hi, mid, lo)


def _lane_fields(shape, fields, first=0):
    lane = lax.broadcasted_iota(jnp.int32, shape, 1)
    out = jnp.zeros(shape, F32)
    for k, val in enumerate(fields):
        out = jnp.where(lane == first + k, val, out)
    return out


def _lane_block_sum(p):
    out = p[:, 0:128]
    for kb in range(1, p.shape[1] // 128):
        out = out + p[:, kb * 128:(kb + 1) * 128]
    return out


def _diff_attn_kernel(q_ref, k_ref, v_ref, slope_ref, lam_ref, ng_ref, o_ref,
                      qs_ref, bias_ref, kmax_ref, m_ref, l_ref, acc_ref, qx_ref, kx_ref, qxs_ref,
                      *, lam_init, tk, chunk):
    i = pl.program_id(1)
    tq = q_ref.shape[0]
    nkv = k_ref.shape[0] // tk
    hd = DA_HEAD_DIM
    c = slope_ref[0][0:1, 0:1]

    @pl.when(i == 0)
    def _():
        a = lax.broadcasted_iota(jnp.int32, (tq, tk), 0)
        b = lax.broadcasted_iota(jnp.int32, (tq, tk), 1)
        d = (a - b).astype(F32) * c
        bias_ref[0] = -d
        bias_ref[1] = d
        bias_ref[2] = -jnp.abs(d)
        rowk = lax.broadcasted_iota(jnp.int32, (tk, 1), 0).astype(F32) * c
        kx_ref[...] = _lane_fields((tk, 128), (1.0,) * 3 + _split3(rowk) + (1.0,)).astype(BF16)
        rowq = lax.broadcasted_iota(jnp.int32, (tq, 1), 0).astype(F32) * c
        for side_idx, sgn in enumerate((1.0, -1.0)):
            qxs_ref[side_idx] = _lane_fields((tq, 128), _split3(-sgn * rowq) + (sgn,) * 3)

        def body(r, acc):
            rows = k_ref[pl.ds(pl.multiple_of(r * chunk, chunk), chunk), :]
            return jnp.maximum(acc, _max_sq_norm(rows))
        kmax = lax.fori_loop(0, k_ref.shape[0] // chunk, body, jnp.zeros((1, 1), F32))
        kmax_ref[...] = jnp.broadcast_to(kmax, kmax_ref.shape)

    qs = _scaled_q(q_ref)
    qs_ref[...] = qs
    bound = jnp.sqrt(_max_sq_norm(qs) * kmax_ref[0:1, 0:1])

    def tile(j):
        off = pl.multiple_of(j * tk, tk)
        cij = -c * jnp.abs(i * tq - j * tk).astype(F32)
        return k_ref[pl.ds(off, tk), :], v_ref[pl.ds(off, tk), :], cij

    def scores(mp, kt, bias):
        sl = slice(mp * hd, (mp + 1) * hd)
        return _dot_nt(qs_ref[:, sl], kt[:, sl]) + bias

    kt, vt, _ = tile(i)
    ps, m_min = [], None
    for mp in range(2):
        s = scores(mp, kt, bias_ref[2])
        m = jnp.max(s, axis=-1, keepdims=True).astype(BF16).astype(F32)
        p = jnp.exp(s - m)
        m_ref[mp] = m
        l_ref[mp] = _lane_block_sum(p)
        ps.append(p.astype(BF16))
        m_low = jnp.min(m, axis=0, keepdims=True)
        m_min = m_low if m_min is None else jnp.minimum(m_min, m_low)
    acc_ref[...] = _dot(jnp.concatenate(ps, axis=0), vt)

    rise = bound - m_min
    w = jnp.floor((rise + DA_SKIP_MARGIN) / (c * tk)) + 1.0
    w = jnp.minimum(w, float(nkv)).astype(jnp.int32)[0, 0]
    fast = jnp.where(rise < DA_FAST_RISE, 1, 0)[0, 0]
    lo = jnp.maximum(i - w, 0)
    n_other = jnp.minimum(i + w, nkv - 1) - lo

    def other_tile(t):
        j = lo + t
        j = j + jnp.where(j >= i, 1, 0)
        side = jnp.where(j < i, 0, 1)
        return tile(j) + (side,)

    for mp in range(2):
        m_cols = _lane_fields((tq, 128), (-m_ref[mp],), first=6)
        for side_idx in range(2):
            qx_ref[2 * side_idx + mp] = (qxs_ref[side_idx] + m_cols).astype(BF16)

    def fast_tiles(ts):
        probs, vals = [], []
        lsum = [None, None]
        for t in ts:
            kt, vt, cij, side = other_tile(t)
            ps = []
            for mp in range(2):
                sl = slice(mp * hd, (mp + 1) * hd)
                qcat = jnp.concatenate([qs_ref[:, sl], qx_ref[2 * side + mp]], axis=1)
                kcat = jnp.concatenate([kt[:, sl], kx_ref[...]], axis=1)
                p = jnp.exp(_dot_nt(qcat, kcat) + cij)
                part = _lane_block_sum(p)
                lsum[mp] = part if lsum[mp] is None else lsum[mp] + part
                ps.append(p.astype(BF16))
            probs.append(jnp.concatenate(ps, axis=0))
            vals.append(vt)
        for mp in range(2):
            l_ref[mp] += lsum[mp]
        acc_ref[...] += _dot(jnp.concatenate(probs, axis=1), jnp.concatenate(vals, axis=0))


    def safe_body(t, carry):
        kt, vt, cij, side = other_tile(t)
        bias = bias_ref[side]
        ps, alphas = [], []
        for mp in range(2):
            s = scores(mp, kt, bias)
            m_old = m_ref[mp]
            m_new = jnp.maximum(m_old, jnp.max(s, axis=-1, keepdims=True) + cij)
            alpha = jnp.exp(m_old - m_new)
            p = jnp.exp(s - (m_new - cij))
            l_ref[mp] = alpha * l_ref[mp] + _lane_block_sum(p)
            m_ref[mp] = m_new
            ps.append(p.astype(BF16))
            alphas.append(alpha)
        pv = _dot(jnp.concatenate(ps, axis=0), vt)
        for mp in range(2):
            rows = slice(mp * tq, (mp + 1) * tq)
            acc_ref[rows, :] = alphas[mp] * acc_ref[rows, :] + pv[rows]
        return carry

    n_fast = jnp.where(fast > 0, n_other, 0)
    start = 0
    for group in DA_FAST_GROUPS:
        count = (n_fast - start) // group

        def fast_group(t, carry, start=start, group=group):
            fast_tiles(tuple(start + group * t + g for g in range(group)))
            return carry

        lax.fori_loop(0, count, fast_group, 0)
        start = start + count * group
    lax.fori_loop(0, n_other - n_fast, safe_body, 0)

    lp = lam_ref[...]
    lam = (jnp.exp(jnp.sum(lp[0:1] * lp[1:2], axis=-1, keepdims=True))
           - jnp.exp(jnp.sum(lp[2:3] * lp[3:4], axis=-1, keepdims=True)) + lam_init)
    l0 = jnp.sum(l_ref[0], axis=-1, keepdims=True)
    l1 = jnp.sum(l_ref[1], axis=-1, keepdims=True)
    o = acc_ref[0:tq, :] / l0 - lam * (acc_ref[tq:2 * tq, :] / l1)
    y = o * lax.rsqrt(jnp.mean(o * o, axis=-1, keepdims=True) + EPS) * ng_ref[...]
    o_ref[...] = (y * (1.0 - lam_init)).astype(o_ref.dtype)


def _diff_attention(proj, lam_params, norm_g, layer, d_hgrn, d_diff, *, tile=512):
    t = proj.shape[0]
    heads = d_diff // DA_V_DIM
    tq = tk = min(tile, t)
    assert t % tq == 0 and tk % 128 == 0
    nq = t // tq
    lam_init = 0.8 - 0.6 * math.exp(-0.3 * layer)
    base = 5 * d_hgrn // DA_V_DIM
    slopes = np.exp2(-8.0 * np.arange(1, heads + 1, dtype=np.float32) / heads).astype(np.float32)
    slopes = jnp.asarray(np.broadcast_to(slopes[:, None, None], (heads, 1, 128)))

    return pl.pallas_call(
        functools.partial(_diff_attn_kernel, lam_init=lam_init, tk=tk, chunk=min(1024, t)),
        grid=(heads, nq),
        in_specs=[pl.BlockSpec((tq, DA_V_DIM), lambda h, i: (i, base + h)),
                  pl.BlockSpec((t, DA_V_DIM), lambda h, i: (0, base + heads + h)),
                  pl.BlockSpec((t, DA_V_DIM), lambda h, i: (0, base + 2 * heads + h)),
                  pl.BlockSpec((1, 1, 128), lambda h, i: (h, 0, 0)),
                  pl.BlockSpec((4, DA_HEAD_DIM), lambda h, i: (0, 0)),
                  pl.BlockSpec((1, DA_V_DIM), lambda h, i: (0, 0))],
        out_specs=pl.BlockSpec((tq, DA_V_DIM), lambda h, i: (i, h)),
        out_shape=jax.ShapeDtypeStruct((t, d_diff), BF16),
        scratch_shapes=[pltpu.VMEM((tq, DA_V_DIM), BF16),
                        pltpu.VMEM((3, tq, tk), F32),
                        pltpu.VMEM((8, 128), F32),
                        pltpu.VMEM((2, tq, 1), F32),
                        pltpu.VMEM((2, tq, 128), F32),
                        pltpu.VMEM((2 * tq, DA_V_DIM), F32),
                        pltpu.VMEM((4, tq, 128), BF16),
                        pltpu.VMEM((tk, 128), BF16),
                        pltpu.VMEM((2, tq, 128), F32)],
        compiler_params=_params("parallel", "arbitrary"),
    )(proj, proj, proj, slopes, lam_params, norm_g.reshape(1, DA_V_DIM))


def _up_conv_glu_kernel(h_ref, hp_ref, hn_ref, wg_ref, wv_ref, cwg_ref, cwv_ref, cbg_ref, cbv_ref,
                        o_ref, hext_ref):
    i, j = pl.program_id(0), pl.program_id(1)
    tm = h_ref.shape[0]
    ext = hext_ref.shape[0]

    @pl.when(j == 0)
    def _():
        hext_ref[HALO:HALO + tm, :] = h_ref[...]

        @pl.when(i > 0)
        def _():
            hext_ref[0:HALO, :] = hp_ref[...]

        @pl.when(i == 0)
        def _():
            hext_ref[0:HALO, :] = jnp.zeros_like(hp_ref)

        @pl.when(i < pl.num_programs(0) - 1)
        def _():
            hext_ref[HALO + tm:ext, :] = hn_ref[...]

        @pl.when(i == pl.num_programs(0) - 1)
        def _():
            hext_ref[HALO + tm:ext, :] = jnp.zeros_like(hn_ref)

    hext = hext_ref[...]

    def conv(w_ref, cw_ref, cb_ref):
        u = _dot(hext, w_ref[...])
        up = pltpu.roll(u, 1, axis=0)[HALO:HALO + tm]
        un = pltpu.roll(u, ext - 1, axis=0)[HALO:HALO + tm]
        cw = cw_ref[...]
        return up * cw[0:1] + u[HALO:HALO + tm] * cw[1:2] + un * cw[2:3] + cb_ref[...]

    gate = conv(wg_ref, cwg_ref, cbg_ref)
    val = conv(wv_ref, cwv_ref, cbv_ref)
    o_ref[...] = (gate * _sigmoid(gate) * val).astype(o_ref.dtype)


def _up_conv_glu(h, w_up, conv_w, conv_b, *, tm=1024, tn=512):
    t, d = h.shape
    n2 = w_up.shape[1]
    dff = n2 // 2
    tm, tn = min(tm, t), min(tn, dff)
    assert t % tm == 0 and dff % tn == 0 and tm % HALO == 0
    ncb = dff // tn
    rh = tm // HALO
    last = t // HALO - 1
    grid = (t // tm, ncb)

    def wcol(off):
        return pl.BlockSpec((d, tn), lambda i, j: (0, off + j))

    def prm(rows, off):
        return pl.BlockSpec((rows, tn), lambda i, j: (0, off + j))

    cb = conv_b.reshape(1, n2)
    return pl.pallas_call(
        _up_conv_glu_kernel,
        grid=grid,
        in_specs=[pl.BlockSpec((tm, d), lambda i, j: (i, 0)),
                  pl.BlockSpec((HALO, d), lambda i, j: (jnp.maximum(i * rh - 1, 0), 0)),
                  pl.BlockSpec((HALO, d), lambda i, j: (jnp.minimum((i + 1) * rh, last), 0)),
                  wcol(0), wcol(ncb),
                  prm(CONV_WIDTH, 0), prm(CONV_WIDTH, ncb), prm(1, 0), prm(1, ncb)],
        out_specs=pl.BlockSpec((tm, tn), lambda i, j: (i, j)),
        out_shape=jax.ShapeDtypeStruct((t, dff), BF16),
        scratch_shapes=[pltpu.VMEM((tm + 2 * HALO, d), BF16)],
        compiler_params=_params("parallel", "arbitrary"),
    )(h, h, h, w_up, w_up, conv_w, conv_w, cb, cb)


def kernel(x, c, w_ada, b_ada, ada_table, norm1_g, w_in, hg_lb_logits, hg_norm_g, da_lambda, da_norm_g,
           w_out, norm2_g, w_up, conv_w, conv_b, w_down, final_g):
    bsz, t, d = x.shape
    assert bsz == 1
    depth = w_in.shape[0]
    d_hgrn = hg_lb_logits.shape[-1]
    d_diff = w_out.shape[1] - d_hgrn
    assert w_in.shape[2] == 5 * d_hgrn + 3 * d_diff

    mod = _ada_mod(c, w_ada, b_ada, ada_table)
    wb_in, wb_out, wb_up, wb_down = (w[0].astype(BF16) for w in (w_in, w_out, w_up, w_down))
    xs = x.reshape(t, d)
    for l in range(depth):
        more = l + 1 < depth

        def cast(*ws):
            return [(w, l + 1) for w in ws] if more else []

        m = [mod[l, i * d:(i + 1) * d] for i in range(N_MOD)]
        h = _rmsnorm(xs, norm1_g[l], m[0], m[1])
        proj, nxt_in_up = _matmul([h], wb_in, tm=1024, tn=1024, cast=cast(w_in, w_up))
        o_fwd = _hgrn_pass(proj, hg_lb_logits, l, d_hgrn, reverse=False)
        o_hg = _hgrn_pass(proj, hg_lb_logits, l, d_hgrn, reverse=True, o_fwd=o_fwd, norm_g=hg_norm_g[l])
        o_da = _diff_attention(proj, da_lambda[l], da_norm_g[l], l, d_hgrn, d_diff)
        if d_hgrn == d_diff:
            mix = [o_hg, o_da]
        else:
            mix = [jnp.concatenate([o_hg, o_da], axis=1)]
        xs, nxt_out = _matmul(mix, wb_out, tm=1024, tn=512, x=xs, gate=m[2], cast=cast(w_out))
        h = _rmsnorm(xs, norm2_g[l], m[3], m[4])
        act = _up_conv_glu(h, wb_up, conv_w[l], conv_b[l])
        xs, nxt_down = _matmul([act], wb_down, tm=1024, tn=256, x=xs, gate=m[5], cast=cast(w_down))
        if more:
            (wb_in, wb_up), (wb_out,), (wb_down,) = nxt_in_up, nxt_out, nxt_down
    out = _rmsnorm(xs, final_g, out_dtype=F32)
    return out.reshape(bsz, t, d)
```

```python
import functools
import math

import jax
import jax.numpy as jnp
import numpy as np
from jax import lax
from jax.experimental import pallas as pl
from jax.experimental.pallas import tpu as pltpu

F32 = jnp.float32
BF16 = jnp.bfloat16

HG_HEAD_DIM = 128
DA_HEAD_DIM = 128
DA_V_DIM = 2 * DA_HEAD_DIM
N_MOD = 6
CONV_WIDTH = 3
NORM_CHUNK = 512
HALO = 16
EPS = 1e-6
HG_LEAF = 16
HG_LEAF_CLAMP = 80.0
DA_SKIP_MARGIN = 104.5
DA_FAST_RISE = 60.0
DA_FAST_GROUPS = (6, 2, 1)
VMEM_LIMIT_BYTES = 56 * 1024 * 1024


def _params(*semantics):
    return pltpu.CompilerParams(dimension_semantics=semantics, vmem_limit_bytes=VMEM_LIMIT_BYTES)


def _dot(a, b):
    return lax.dot_general(a, b, (((1,), (0,)), ((), ())), preferred_element_type=F32)


def _dot_nt(a, b):
    return lax.dot_general(a, b, (((1,), (1,)), ((), ())), preferred_element_type=F32)


def _dot_tn(a, b):
    return lax.dot_general(a, b, (((0,), (0,)), ((), ())), preferred_element_type=F32)


def _sigmoid(x):
    return 1.0 / (1.0 + jnp.exp(-x))


def _pick(n, *cands):
    for c in cands:
        if n % c == 0:
            return c
    return n


def _mod_kernel(c_ref, w_ref, b_ref, t_ref, o_ref):
    c = c_ref[...]
    s = c * _sigmoid(c)
    y = jnp.sum(s * w_ref[...], axis=0, keepdims=True)
    o_ref[...] = y + b_ref[...] + t_ref[...]


def _ada_mod(c, w_ada, b_ada, ada_table):
    _, d = c.shape
    depth = ada_table.shape[0]
    n = w_ada.shape[1]
    tn = _pick(n, 512, 256, 128)
    return pl.pallas_call(
        _mod_kernel,
        grid=(n // tn,),
        in_specs=[pl.BlockSpec((d, 1), lambda j: (0, 0)),
                  pl.BlockSpec((d, tn), lambda j: (0, j)),
                  pl.BlockSpec((1, tn), lambda j: (0, j)),
                  pl.BlockSpec((depth, tn), lambda j: (0, j))],
        out_specs=pl.BlockSpec((depth, tn), lambda j: (0, j)),
        out_shape=jax.ShapeDtypeStruct((depth, n), F32),
        compiler_params=_params("parallel"),
    )(c.reshape(d, 1), w_ada, b_ada.reshape(1, n), ada_table.reshape(depth, n))


def _norm_kernel(x_ref, g_ref, *rest, modulated):
    if modulated:
        shift_ref, scale_ref, o_ref = rest
    else:
        (o_ref,) = rest
    d = x_ref.shape[1]
    chunk = min(NORM_CHUNK, d)
    part = None
    for c0 in range(0, d, chunk):
        xc = x_ref[:, c0:c0 + chunk]
        sq = _lane_block_sum(xc * xc)
        part = sq if part is None else part + sq
    inv = lax.rsqrt(jnp.sum(part, axis=-1, keepdims=True) * (1.0 / d) + EPS)
    for c0 in range(0, d, chunk):
        cols = slice(c0, c0 + chunk)
        gain = g_ref[:, cols]
        if modulated:
            gain = gain * (1.0 + scale_ref[:, cols])
        y = x_ref[:, cols] * inv * gain
        if modulated:
            y = y + shift_ref[:, cols]
        o_ref[:, cols] = y.astype(o_ref.dtype)


def _rmsnorm(x, g, shift=None, scale=None, out_dtype=BF16):
    t, d = x.shape
    tr = _pick(t, 256, 128, 64, 32, 16, 8)
    modulated = shift is not None
    row = pl.BlockSpec((1, d), lambda i: (0, 0))
    args = [x, g.reshape(1, d)] + ([shift.reshape(1, d), scale.reshape(1, d)] if modulated else [])
    return pl.pallas_call(
        functools.partial(_norm_kernel, modulated=modulated),
        grid=(t // tr,),
        in_specs=[pl.BlockSpec((tr, d), lambda i: (i, 0))] + [row] * (len(args) - 1),
        out_specs=pl.BlockSpec((tr, d), lambda i: (i, 0)),
        out_shape=jax.ShapeDtypeStruct((t, d), out_dtype),
        compiler_params=_params("parallel"),
    )(*args)


def _cast_job(w_all, layer, grid):
    _, k, n = w_all.shape
    gi, gj = grid
    assert k % gi == 0 and n % gj == 0
    bk, bn = k // gi, n // gj
    assert bk % 16 == 0 and bn % 128 == 0
    return (pl.BlockSpec((None, bk, bn), lambda i, j: (layer, i, j)),
            pl.BlockSpec((bk, bn), lambda i, j: (i, j)),
            jax.ShapeDtypeStruct((k, n), BF16))


def _matmul_kernel(*refs, n_a, residual, n_cast):
    a_refs = refs[:n_a]
    w_ref = refs[n_a]
    n_in = n_a + 1 + (2 if residual else 0) + n_cast
    o_ref = refs[n_in]
    ka = a_refs[0].shape[1]
    acc = _dot(a_refs[0][...], w_ref[0:ka, :])
    for p in range(1, n_a):
        acc = acc + _dot(a_refs[p][...], w_ref[p * ka:(p + 1) * ka, :])
    if residual:
        x_ref, gate_ref = refs[n_a + 1], refs[n_a + 2]
        o_ref[...] = x_ref[...] + gate_ref[...] * acc
    else:
        o_ref[...] = acc.astype(o_ref.dtype)
    for c in range(n_cast):
        refs[n_in + 1 + c][...] = refs[n_in - n_cast + c][...].astype(BF16)


def _matmul(a_list, w, *, tm, tn, out_dtype=BF16, x=None, gate=None, cast=()):
    m, ka = a_list[0].shape
    n_a = len(a_list)
    kt, n = w.shape
    assert kt == n_a * ka and all(a.shape == (m, ka) for a in a_list)
    tm, tn = min(tm, m), min(tn, n)
    assert m % tm == 0 and n % tn == 0
    grid = (m // tm, n // tn)
    residual = x is not None
    in_specs = [pl.BlockSpec((tm, ka), lambda i, j: (i, 0)) for _ in range(n_a)]
    in_specs.append(pl.BlockSpec((kt, tn), lambda i, j: (0, j)))
    args = list(a_list) + [w]
    if residual:
        in_specs += [pl.BlockSpec((tm, tn), lambda i, j: (i, j)),
                     pl.BlockSpec((1, tn), lambda i, j: (0, j))]
        args += [x, gate.reshape(1, n)]
        out_dtype = F32
    out_specs = [pl.BlockSpec((tm, tn), lambda i, j: (i, j))]
    out_shape = [jax.ShapeDtypeStruct((m, n), out_dtype)]
    for w_all, layer in cast:
        c_in, c_out, c_shape = _cast_job(w_all, layer, grid)
        in_specs.append(c_in)
        args.append(w_all)
        out_specs.append(c_out)
        out_shape.append(c_shape)
    res = pl.pallas_call(
        functools.partial(_matmul_kernel, n_a=n_a, residual=residual, n_cast=len(cast)),
        grid=grid,
        in_specs=in_specs,
        out_specs=out_specs,
        out_shape=out_shape,
        compiler_params=_params("parallel", "parallel"),
    )(*args)
    return res[0], list(res[1:])


def _hgrn_level_ids(r, reverse):
    t = np.arange(r)[:, None]
    s = np.arange(r)[None, :]
    ids = np.full((r, r), -1, np.int32)
    tri = (s >= t) if reverse else (s <= t)
    ids[(t // HG_LEAF == s // HG_LEAF) & tri] = 0
    lv, i = HG_LEAF, 1
    while 2 * lv <= r:
        same = (t // (2 * lv)) == (s // (2 * lv))
        t_hi, s_hi = (t & lv) != 0, (s & lv) != 0
        pair = (~t_hi & s_hi) if reverse else (t_hi & ~s_hi)
        ids[same & pair] = i
        lv, i = 2 * lv, i + 1
    return ids


def _hgrn_kernel(q_ref, z_ref, v_ref, lbl_ref, ids_ref, *rest, reverse, layer, heads, final):
    if final:
        ofwd_ref, gate_ref, ng_ref, o_ref, st_ref = rest
    else:
        o_ref, st_ref = rest
    r = q_ref.shape[0]
    hd = HG_HEAD_DIM

    @pl.when(pl.program_id(1) == 0)
    def _():
        st_ref[...] = jnp.zeros_like(st_ref)

    logits = lbl_ref[0]
    e = jnp.exp(logits - jnp.max(logits, axis=0, keepdims=True))
    lb_all = (jnp.sum(e[1:layer + 1], axis=0, keepdims=True) / jnp.sum(e, axis=0, keepdims=True)
              if layer > 0 else jnp.zeros_like(e[0:1]))

    ids = ids_ref[...]
    n_levels = (r // HG_LEAF).bit_length()
    masks = [ids == level for level in range(n_levels)]
    row = lax.broadcasted_iota(jnp.int32, (r, r), 0)
    col = lax.broadcasted_iota(jnp.int32, (r, r), 1)
    tri = jnp.where((col >= row) if reverse else (col <= row), 1.0, 0.0).astype(BF16)
    total_row = 0 if reverse else r - 1

    for h in range(heads):
        sl = slice(h * hd, (h + 1) * hd)
        lb = lb_all[:, sl]
        q = q_ref[:, sl].astype(F32) * (hd ** -0.5)
        z = z_ref[:, sl].astype(F32)
        v = v_ref[:, sl]
        sig = _sigmoid(z)
        g = jnp.log(lb + (1.0 - lb) * sig)
        kc = (1.0 - lb) * (1.0 - sig)

        g1 = g.astype(BF16)
        rem = g - g1.astype(F32)
        g2 = rem.astype(BF16)
        g3 = (rem - g2.astype(F32)).astype(BF16)
        b = _dot(tri, g1) + _dot(tri, g2) + _dot(tri, g3)

        a = jnp.zeros((r, r), F32)
        lv, level = HG_LEAF // 2, 0
        while 2 * lv <= r:
            blk = 2 * lv
            nb = r // blk
            ref_row = lv if reverse else lv - 1
            bb = b.reshape(nb, blk, hd)
            d = bb - bb[:, ref_row:ref_row + 1, :]
            if level == 0:
                eq = jnp.exp(jnp.minimum(d, HG_LEAF_CLAMP))
                ek = jnp.exp(jnp.minimum(-d, HG_LEAF_CLAMP))
            else:
                eq = ek = jnp.exp(-jnp.abs(d))
            qe = (q.reshape(nb, blk, hd) * eq).reshape(r, hd).astype(BF16)
            ke = (kc.reshape(nb, blk, hd) * ek).reshape(r, hd).astype(BF16)
            a = jnp.where(masks[level], _dot_nt(qe, ke), a)
            lv, level = 2 * lv, level + 1
        assert level == n_levels

        st = st_ref[h]
        o = _dot(a.astype(BF16), v) + _dot_nt((q * jnp.exp(b)).astype(BF16), st.astype(BF16))
        bl = b[total_row:total_row + 1, :]
        kd = (kc * jnp.exp(bl - b)).astype(BF16)
        st_ref[h] = jnp.exp(bl) * st + _dot_tn(v, kd)

        if final:
            o = o + ofwd_ref[:, sl]
            y = o * lax.rsqrt(jnp.mean(o * o, axis=-1, keepdims=True) + EPS) * ng_ref[...]
            gt = gate_ref[:, sl].astype(F32)
            o_ref[:, sl] = (y * (gt * _sigmoid(gt))).astype(o_ref.dtype)
        else:
            o_ref[:, sl] = o


def _hgrn_pass(proj, lb_logits, layer, d_hgrn, *, reverse, o_fwd=None, norm_g=None, chunk=256):
    t = proj.shape[0]
    heads_total = d_hgrn // HG_HEAD_DIM
    hb = _pick(heads_total, 16, 8, 4, 2, 1)
    w = hb * HG_HEAD_DIM
    r = min(chunk, t)
    assert t % r == 0 and r % (2 * HG_LEAF) == 0
    nchunks = t // r
    cpb = d_hgrn // w
    final = o_fwd is not None
    depth = lb_logits.shape[1]

    def rows(n):
        return (nchunks - 1 - n) if reverse else n

    def col_spec(group):
        return pl.BlockSpec((r, w), lambda h, n: (rows(n), group * cpb + h))

    in_specs = [col_spec(0), col_spec(2 if reverse else 1), col_spec(3),
                pl.BlockSpec((1, depth, w), lambda h, n: (1 if reverse else 0, 0, h)),
                pl.BlockSpec((r, r), lambda h, n: (0, 0))]
    args = [proj, proj, proj, lb_logits, jnp.asarray(_hgrn_level_ids(r, reverse))]
    if final:
        in_specs += [pl.BlockSpec((r, w), lambda h, n: (rows(n), h)), col_spec(4),
                     pl.BlockSpec((1, HG_HEAD_DIM), lambda h, n: (0, 0))]
        args += [o_fwd, proj, norm_g.reshape(1, HG_HEAD_DIM)]
    return pl.pallas_call(
        functools.partial(_hgrn_kernel, reverse=reverse, layer=layer, heads=hb, final=final),
        grid=(heads_total // hb, nchunks),
        in_specs=in_specs,
        out_specs=pl.BlockSpec((r, w), lambda h, n: (rows(n), h)),
        out_shape=jax.ShapeDtypeStruct((t, d_hgrn), BF16 if final else F32),
        scratch_shapes=[pltpu.VMEM((hb, HG_HEAD_DIM, HG_HEAD_DIM), F32)],
        compiler_params=_params("parallel", "arbitrary"),
    )(*args)


def _scaled_q(q_ref):
    return (q_ref[...].astype(F32) * (DA_HEAD_DIM ** -0.5)).astype(BF16)


def _max_sq_norm(x):
    sq = x.astype(F32)
    sq = sq * sq
    n = jnp.maximum(jnp.sum(sq[:, :DA_HEAD_DIM], axis=-1, keepdims=True),
                    jnp.sum(sq[:, DA_HEAD_DIM:], axis=-1, keepdims=True))
    return jnp.max(n, axis=0, keepdims=True)


def _split3(x):
    hi = x.astype(BF16).astype(F32)
    rem = x - hi
    mid = rem.astype(BF16).astype(F32)
    lo = (rem - mid).astype(BF16).astype(F32)
    return (hi, mid, lo)


def _lane_fields(shape, fields, first=0):
    lane = lax.broadcasted_iota(jnp.int32, shape, 1)
    out = jnp.zeros(shape, F32)
    for k, val in enumerate(fields):
        out = jnp.where(lane == first + k, val, out)
    return out


def _lane_block_sum(p):
    out = p[:, 0:128]
    for kb in range(1, p.shape[1] // 128):
        out = out + p[:, kb * 128:(kb + 1) * 128]
    return out


def _diff_attn_kernel(q_ref, k_ref, v_ref, slope_ref, lam_ref, ng_ref, o_ref,
                      qs_ref, bias_ref, kmax_ref, m_ref, l_ref, acc_ref, qx_ref, kx_ref, qxs_ref,
                      *, lam_init, tk, chunk):
    i = pl.program_id(1)
    tq = q_ref.shape[0]
    nkv = k_ref.shape[0] // tk
    hd = DA_HEAD_DIM
    c = slope_ref[0][0:1, 0:1]

    @pl.when(i == 0)
    def _():
        a = lax.broadcasted_iota(jnp.int32, (tq, tk), 0)
        b = lax.broadcasted_iota(jnp.int32, (tq, tk), 1)
        d = (a - b).astype(F32) * c
        bias_ref[0] = -d
        bias_ref[1] = d
        bias_ref[2] = -jnp.abs(d)
        rowk = lax.broadcasted_iota(jnp.int32, (tk, 1), 0).astype(F32) * c
        kx_ref[...] = _lane_fields((tk, 128), (1.0,) * 3 + _split3(rowk) + (1.0,)).astype(BF16)
        rowq = lax.broadcasted_iota(jnp.int32, (tq, 1), 0).astype(F32) * c
        for side_idx, sgn in enumerate((1.0, -1.0)):
            qxs_ref[side_idx] = _lane_fields((tq, 128), _split3(-sgn * rowq) + (sgn,) * 3)

        def body(r, acc):
            rows = k_ref[pl.ds(pl.multiple_of(r * chunk, chunk), chunk), :]
            return jnp.maximum(acc, _max_sq_norm(rows))
        kmax = lax.fori_loop(0, k_ref.shape[0] // chunk, body, jnp.zeros((1, 1), F32))
        kmax_ref[...] = jnp.broadcast_to(kmax, kmax_ref.shape)

    qs = _scaled_q(q_ref)
    qs_ref[...] = qs
    bound = jnp.sqrt(_max_sq_norm(qs) * kmax_ref[0:1, 0:1])

    def tile(j):
        off = pl.multiple_of(j * tk, tk)
        cij = -c * jnp.abs(i * tq - j * tk).astype(F32)
        return k_ref[pl.ds(off, tk), :], v_ref[pl.ds(off, tk), :], cij

    def scores(mp, kt, bias):
        sl = slice(mp * hd, (mp + 1) * hd)
        return _dot_nt(qs_ref[:, sl], kt[:, sl]) + bias

    kt, vt, _ = tile(i)
    ps, m_min = [], None
    for mp in range(2):
        s = scores(mp, kt, bias_ref[2])
        m = jnp.max(s, axis=-1, keepdims=True).astype(BF16).astype(F32)
        p = jnp.exp(s - m)
        m_ref[mp] = m
        l_ref[mp] = _lane_block_sum(p)
        ps.append(p.astype(BF16))
        m_low = jnp.min(m, axis=0, keepdims=True)
        m_min = m_low if m_min is None else jnp.minimum(m_min, m_low)
    acc_ref[...] = _dot(jnp.concatenate(ps, axis=0), vt)

    rise = bound - m_min
    w = jnp.floor((rise + DA_SKIP_MARGIN) / (c * tk)) + 1.0
    w = jnp.minimum(w, float(nkv)).astype(jnp.int32)[0, 0]
    fast = jnp.where(rise < DA_FAST_RISE, 1, 0)[0, 0]
    lo = jnp.maximum(i - w, 0)
    n_other = jnp.minimum(i + w, nkv - 1) - lo

    def other_tile(t):
        j = lo + t
        j = j + jnp.where(j >= i, 1, 0)
        side = jnp.where(j < i, 0, 1)
        return tile(j) + (side,)

    for mp in range(2):
        m_cols = _lane_fields((tq, 128), (-m_ref[mp],), first=6)
        for side_idx in range(2):
            qx_ref[2 * side_idx + mp] = (qxs_ref[side_idx] + m_cols).astype(BF16)

    def fast_tiles(ts):
        probs, vals = [], []
        lsum = [None, None]
        for t in ts:
            kt, vt, cij, side = other_tile(t)
            ps = []
            for mp in range(2):
                sl = slice(mp * hd, (mp + 1) * hd)
                qcat = jnp.concatenate([qs_ref[:, sl], qx_ref[2 * side + mp]], axis=1)
                kcat = jnp.concatenate([kt[:, sl], kx_ref[...]], axis=1)
                p = jnp.exp(_dot_nt(qcat, kcat) + cij)
                part = _lane_block_sum(p)
                lsum[mp] = part if lsum[mp] is None else lsum[mp] + part
                ps.append(p.astype(BF16))
            probs.append(jnp.concatenate(ps, axis=0))
            vals.append(vt)
        for mp in range(2):
            l_ref[mp] += lsum[mp]
        acc_ref[...] += _dot(jnp.concatenate(probs, axis=1), jnp.concatenate(vals, axis=0))


    def safe_body(t, carry):
        kt, vt, cij, side = other_tile(t)
        bias = bias_ref[side]
        ps, alphas = [], []
        for mp in range(2):
            s = scores(mp, kt, bias)
            m_old = m_ref[mp]
            m_new = jnp.maximum(m_old, jnp.max(s, axis=-1, keepdims=True) + cij)
            alpha = jnp.exp(m_old - m_new)
            p = jnp.exp(s - (m_new - cij))
            l_ref[mp] = alpha * l_ref[mp] + _lane_block_sum(p)
            m_ref[mp] = m_new
            ps.append(p.astype(BF16))
            alphas.append(alpha)
        pv = _dot(jnp.concatenate(ps, axis=0), vt)
        for mp in range(2):
            rows = slice(mp * tq, (mp + 1) * tq)
            acc_ref[rows, :] = alphas[mp] * acc_ref[rows, :] + pv[rows]
        return carry

    n_fast = jnp.where(fast > 0, n_other, 0)
    start = 0
    for group in DA_FAST_GROUPS:
        count = (n_fast - start) // group

        def fast_group(t, carry, start=start, group=group):
            fast_tiles(tuple(start + group * t + g for g in range(group)))
            return carry

        lax.fori_loop(0, count, fast_group, 0)
        start = start + count * group
    lax.fori_loop(0, n_other - n_fast, safe_body, 0)

    lp = lam_ref[...]
    lam = (jnp.exp(jnp.sum(lp[0:1] * lp[1:2], axis=-1, keepdims=True))
           - jnp.exp(jnp.sum(lp[2:3] * lp[3:4], axis=-1, keepdims=True)) + lam_init)
    l0 = jnp.sum(l_ref[0], axis=-1, keepdims=True)
    l1 = jnp.sum(l_ref[1], axis=-1, keepdims=True)
    o = acc_ref[0:tq, :] / l0 - lam * (acc_ref[tq:2 * tq, :] / l1)
    y = o * lax.rsqrt(jnp.mean(o * o, axis=-1, keepdims=True) + EPS) * ng_ref[...]
    o_ref[...] = (y * (1.0 - lam_init)).astype(o_ref.dtype)


def _diff_attention(proj, lam_params, norm_g, layer, d_hgrn, d_diff, *, tile=512):
    t = proj.shape[0]
    heads = d_diff // DA_V_DIM
    tq = tk = min(tile, t)
    assert t % tq == 0 and tk % 128 == 0
    nq = t // tq
    lam_init = 0.8 - 0.6 * math.exp(-0.3 * layer)
    base = 5 * d_hgrn // DA_V_DIM
    slopes = np.exp2(-8.0 * np.arange(1, heads + 1, dtype=np.float32) / heads).astype(np.float32)
    slopes = jnp.asarray(np.broadcast_to(slopes[:, None, None], (heads, 1, 128)))

    return pl.pallas_call(
        functools.partial(_diff_attn_kernel, lam_init=lam_init, tk=tk, chunk=min(1024, t)),
        grid=(heads, nq),
        in_specs=[pl.BlockSpec((tq, DA_V_DIM), lambda h, i: (i, base + h)),
                  pl.BlockSpec((t, DA_V_DIM), lambda h, i: (0, base + heads + h)),
                  pl.BlockSpec((t, DA_V_DIM), lambda h, i: (0, base + 2 * heads + h)),
                  pl.BlockSpec((1, 1, 128), lambda h, i: (h, 0, 0)),
                  pl.BlockSpec((4, DA_HEAD_DIM), lambda h, i: (0, 0)),
                  pl.BlockSpec((1, DA_V_DIM), lambda h, i: (0, 0))],
        out_specs=pl.BlockSpec((tq, DA_V_DIM), lambda h, i: (i, h)),
        out_shape=jax.ShapeDtypeStruct((t, d_diff), BF16),
        scratch_shapes=[pltpu.VMEM((tq, DA_V_DIM), BF16),
                        pltpu.VMEM((3, tq, tk), F32),
                        pltpu.VMEM((8, 128), F32),
                        pltpu.VMEM((2, tq, 1), F32),
                        pltpu.VMEM((2, tq, 128), F32),
                        pltpu.VMEM((2 * tq, DA_V_DIM), F32),
                        pltpu.VMEM((4, tq, 128), BF16),
                        pltpu.VMEM((tk, 128), BF16),
                        pltpu.VMEM((2, tq, 128), F32)],
        compiler_params=_params("parallel", "arbitrary"),
    )(proj, proj, proj, slopes, lam_params, norm_g.reshape(1, DA_V_DIM))


def _up_conv_glu_kernel(h_ref, hp_ref, hn_ref, wg_ref, wv_ref, cwg_ref, cwv_ref, cbg_ref, cbv_ref,
                        o_ref, hext_ref):
    i, j = pl.program_id(0), pl.program_id(1)
    tm = h_ref.shape[0]
    ext = hext_ref.shape[0]

    @pl.when(j == 0)
    def _():
        hext_ref[HALO:HALO + tm, :] = h_ref[...]

        @pl.when(i > 0)
        def _():
            hext_ref[0:HALO, :] = hp_ref[...]

        @pl.when(i == 0)
        def _():
            hext_ref[0:HALO, :] = jnp.zeros_like(hp_ref)

        @pl.when(i < pl.num_programs(0) - 1)
        def _():
            hext_ref[HALO + tm:ext, :] = hn_ref[...]

        @pl.when(i == pl.num_programs(0) - 1)
        def _():
            hext_ref[HALO + tm:ext, :] = jnp.zeros_like(hn_ref)

    hext = hext_ref[...]

    def conv(w_ref, cw_ref, cb_ref):
        u = _dot(hext, w_ref[...])
        up = pltpu.roll(u, 1, axis=0)[HALO:HALO + tm]
        un = pltpu.roll(u, ext - 1, axis=0)[HALO:HALO + tm]
        cw = cw_ref[...]
        return up * cw[0:1] + u[HALO:HALO + tm] * cw[1:2] + un * cw[2:3] + cb_ref[...]

    gate = conv(wg_ref, cwg_ref, cbg_ref)
    val = conv(wv_ref, cwv_ref, cbv_ref)
    o_ref[...] = (gate * _sigmoid(gate) * val).astype(o_ref.dtype)


def _up_conv_glu(h, w_up, conv_w, conv_b, *, tm=1024, tn=512):
    t, d = h.shape
    n2 = w_up.shape[1]
    dff = n2 // 2
    tm, tn = min(tm, t), min(tn, dff)
    assert t % tm == 0 and dff % tn == 0 and tm % HALO == 0
    ncb = dff // tn
    rh = tm // HALO
    last = t // HALO - 1
    grid = (t // tm, ncb)

    def wcol(off):
        return pl.BlockSpec((d, tn), lambda i, j: (0, off + j))

    def prm(rows, off):
        return pl.BlockSpec((rows, tn), lambda i, j: (0, off + j))

    cb = conv_b.reshape(1, n2)
    return pl.pallas_call(
        _up_conv_glu_kernel,
        grid=grid,
        in_specs=[pl.BlockSpec((tm, d), lambda i, j: (i, 0)),
                  pl.BlockSpec((HALO, d), lambda i, j: (jnp.maximum(i * rh - 1, 0), 0)),
                  pl.BlockSpec((HALO, d), lambda i, j: (jnp.minimum((i + 1) * rh, last), 0)),
                  wcol(0), wcol(ncb),
                  prm(CONV_WIDTH, 0), prm(CONV_WIDTH, ncb), prm(1, 0), prm(1, ncb)],
        out_specs=pl.BlockSpec((tm, tn), lambda i, j: (i, j)),
        out_shape=jax.ShapeDtypeStruct((t, dff), BF16),
        scratch_shapes=[pltpu.VMEM((tm + 2 * HALO, d), BF16)],
        compiler_params=_params("parallel", "arbitrary"),
    )(h, h, h, w_up, w_up, conv_w, conv_w, cb, cb)


def kernel(x, c, w_ada, b_ada, ada_table, norm1_g, w_in, hg_lb_logits, hg_norm_g, da_lambda, da_norm_g,
           w_out, norm2_g, w_up, conv_w, conv_b, w_down, final_g):
    bsz, t, d = x.shape
    assert bsz == 1
    depth = w_in.shape[0]
    d_hgrn = hg_lb_logits.shape[-1]
    d_diff = w_out.shape[1] - d_hgrn
    assert w_in.shape[2] == 5 * d_hgrn + 3 * d_diff

    mod = _ada_mod(c, w_ada, b_ada, ada_table)
    wb_in, wb_out, wb_up, wb_down = (w[0].astype(BF16) for w in (w_in, w_out, w_up, w_down))
    xs = x.reshape(t, d)
    for l in range(depth):
        more = l + 1 < depth

        def cast(*ws):
            return [(w, l + 1) for w in ws] if more else []

        m = [mod[l, i * d:(i + 1) * d] for i in range(N_MOD)]
        h = _rmsnorm(xs, norm1_g[l], m[0], m[1])
        proj, nxt_in_up = _matmul([h], wb_in, tm=1024, tn=1024, cast=cast(w_in, w_up))
        o_fwd = _hgrn_pass(proj, hg_lb_logits, l, d_hgrn, reverse=False)
        o_hg = _hgrn_pass(proj, hg_lb_logits, l, d_hgrn, reverse=True, o_fwd=o_fwd, norm_g=hg_norm_g[l])
        o_da = _diff_attention(proj, da_lambda[l], da_norm_g[l], l, d_hgrn, d_diff)
        if d_hgrn == d_diff:
            mix = [o_hg, o_da]
        else:
            mix = [jnp.concatenate([o_hg, o_da], axis=1)]
        xs, nxt_out = _matmul(mix, wb_out, tm=1024, tn=512, x=xs, gate=m[2], cast=cast(w_out))
        h = _rmsnorm(xs, norm2_g[l], m[3], m[4])
        act = _up_conv_glu(h, wb_up, conv_w[l], conv_b[l])
        xs, nxt_down = _matmul([act], wb_down, tm=1024, tn=256, x=xs, gate=m[5], cast=cast(w_down))
        if more:
            (wb_in, wb_up), (wb_out,), (wb_down,) = nxt_in_up, nxt_out, nxt_down
    out = _rmsnorm(xs, final_g, out_dtype=F32)
    return out.reshape(bsz, t, d)
```
